```python
import math
import jax, jax.numpy as jnp
from jax import lax
import numpy as np

D_MODEL = 1024
BATCH = 16
SEQ = 4096
DEPTH = 1

GLA_HEADS = 4
GLA_K = D_MODEL // 2
GLA_V = D_MODEL
GLA_DK = GLA_K // GLA_HEADS
GLA_DV = GLA_V // GLA_HEADS
GLA_RANK = 16
GLA_GATE_NORM = 16.0
GLA_CHUNK = 64
DIFF_HEADS = 4
DIFF_HD = 128
DIFF_QK = DIFF_HEADS * 2 * DIFF_HD
DIFF_V = DIFF_HEADS * 2 * DIFF_HD
Q_BLOCK = 128
ROPE_THETA = 10000.0
FFN_HIDDEN = -(-8 * D_MODEL // (3 * 256)) * 256
NORM_EPS = 1e-6
SUBLN_EPS = 1e-5
IN_SIZES = (GLA_K, GLA_K, GLA_V, GLA_V, 2 * GLA_RANK, DIFF_QK, DIFF_QK, DIFF_V, D_MODEL, D_MODEL)
IN_WIDTH = sum(IN_SIZES)

kernel_name = "hybrid_gla_diffattn_encoder_block"


def rms_norm(x, w, eps=NORM_EPS):
    xf = x.astype(jnp.float32)
    y = xf * lax.rsqrt(jnp.mean(xf * xf, axis=-1, keepdims=True) + eps)
    return (y * w.astype(jnp.float32)).astype(x.dtype)


def rope(x, pos):
    d = x.shape[-1]
    inv_freq = 1.0 / (ROPE_THETA ** (jnp.arange(0, d, 2, dtype=jnp.float32) / d))
    freqs = pos.astype(jnp.float32)[:, None] * inv_freq[None, :]
    emb = jnp.concatenate([freqs, freqs], axis=-1)
    cos = jnp.cos(emb)[None, :, None, None, :]
    sin = jnp.sin(emb)[None, :, None, None, :]
    xf = x.astype(jnp.float32)
    x1, x2 = xf[..., : d // 2], xf[..., d // 2:]
    rot = jnp.concatenate([-x2, x1], axis=-1)
    return (xf * cos + rot * sin).astype(x.dtype)


def gla_chunked(q, k, v, g):
    B, S, H, dk = q.shape
    dv = v.shape[-1]
    C = GLA_CHUNK
    N = S // C

    def chunks(t):
        return t.reshape(B, N, C, H, t.shape[-1]).transpose(1, 0, 3, 2, 4)

    qc, kc, vc, gc = chunks(q), chunks(k), chunks(v), chunks(g)
    b = jnp.cumsum(gc, axis=3)
    b_last = b[:, :, :, -1:, :]
    q_t = qc * jnp.exp(b)
    k_t = kc * jnp.exp(-b)
    k_s = kc * jnp.exp(b_last - b)
    mask = jnp.tril(jnp.ones((C, C), dtype=bool))
    attn = jnp.where(mask, jnp.einsum('nbhid,nbhjd->nbhij', q_t, k_t), 0.0)
    o_intra = jnp.einsum('nbhij,nbhjv->nbhiv', attn, vc)

    def step(state, xs):
        q_n, k_n, v_n, decay_n = xs
        o = jnp.einsum('bhid,bhdv->bhiv', q_n, state)
        state = state * decay_n[:, :, 0, :, None] + jnp.einsum('bhjd,bhjv->bhdv', k_n, v_n)
        return state, o

    state0 = jnp.zeros((B, H, dk, dv), dtype=q.dtype)
    _, o_inter = lax.scan(step, state0, (q_t, k_s, vc, jnp.exp(b_last)))
    o = o_intra + o_inter
    return o.transpose(1, 0, 3, 2, 4).reshape(B, S, H, dv)


def diff_attention(q, k, v, lam):
    B, S, H, _, d = q.shape
    nb = S // Q_BLOCK
    qb = q.reshape(B, nb, Q_BLOCK, H, 2, d).transpose(1, 0, 2, 3, 4, 5)

    def block(q_blk):
        s = jnp.einsum('bqhcd,bkhcd->bhcqk', q_blk, k).astype(jnp.float32)
        p = jax.nn.softmax(s, axis=-1)
        w = p[:, :, 0] - lam * p[:, :, 1]
        return jnp.einsum('bhqk,bkhv->bqhv', w.astype(v.dtype), v)

    o = lax.map(block, qb)
    return o.transpose(1, 0, 2, 3, 4).reshape(B, S, H, 2 * d)


def hybrid_mixer(u, w_in, w_gk2, b_gk, gla_norm_w, lq1, lk1, lq2, lk2, subln_w, w_out, lambda_init):
    B, S, _ = u.shape
    f32 = jnp.float32
    proj = u @ w_in
    idx = []
    acc = 0
    for sz in IN_SIZES[:-1]:
        acc += sz
        idx.append(acc)
    gq, gk, gv, gr, glr, dq, dk, dv, ga, gb = jnp.split(proj, idx, axis=-1)

    q = (gq.astype(f32) * (GLA_DK ** -0.5)).reshape(B, S, GLA_HEADS, GLA_DK)
    k = gk.astype(f32).reshape(B, S, GLA_HEADS, GLA_DK)
    v = gv.astype(f32).reshape(B, S, GLA_HEADS, GLA_DV)
    glr = glr.astype(f32).reshape(B, S, 2, GLA_RANK)
    logits = jnp.einsum('bsdr,drk->bsdk', glr, w_gk2.astype(f32)) + b_gk.astype(f32)
    log_a = jax.nn.log_sigmoid(logits) / GLA_GATE_NORM
    g_f = log_a[:, :, 0].reshape(B, S, GLA_HEADS, GLA_DK)
    g_b = log_a[:, :, 1].reshape(B, S, GLA_HEADS, GLA_DK)
    o_f = gla_chunked(q, k, v, g_f)
    o_b = jnp.flip(gla_chunked(jnp.flip(q, 1), jnp.flip(k, 1), jnp.flip(v, 1), jnp.flip(g_b, 1)), 1)
    o_a = rms_norm(o_f + o_b, gla_norm_w)
    y_a = o_a.reshape(B, S, GLA_V) * jax.nn.silu(gr.astype(f32))

    pos = jnp.arange(S)
    q2 = rope(dq.reshape(B, S, DIFF_HEADS, 2, DIFF_HD), pos) * jnp.asarray(DIFF_HD ** -0.5, dtype=dq.dtype)
    k2 = rope(dk.reshape(B, S, DIFF_HEADS, 2, DIFF_HD), pos)
    v2 = dv.reshape(B, S, DIFF_HEADS, 2 * DIFF_HD)
    lam = (jnp.exp(jnp.sum(lq1.astype(f32) * lk1.astype(f32)))
           - jnp.exp(jnp.sum(lq2.astype(f32) * lk2.astype(f32))) + lambda_init)
    o2 = diff_attention(q2, k2, v2, lam)
    y_b = (rms_norm(o2, subln_w, eps=SUBLN_EPS).astype(f32) * (1.0 - lambda_init)).reshape(B, S, DIFF_V)

    merged = jax.nn.sigmoid(ga.astype(f32)) * y_a + jax.nn.sigmoid(gb.astype(f32)) * y_b
    return merged.astype(u.dtype) @ w_out


def swiglu(x, w_ffn_in, w_ffn_out):
    a = x @ w_ffn_in
    gate, up = jnp.split(a, [FFN_HIDDEN], axis=-1)
    return (jax.nn.silu(gate) * up) @ w_ffn_out


def setup_inputs(seed: int = 0) -> dict:
    key = jax.random.key(seed)
    ks = jax.random.split(key, 16)
    f = jnp.float32
    nrm = lambda k, shp, s: jax.random.normal(k, shp, dtype=f) * s
    return {
        "x": jax.random.normal(ks[0], (BATCH, SEQ, D_MODEL), dtype=f),
        "norm_mix_w": 1.0 + nrm(ks[1], (DEPTH, D_MODEL), 0.02),
        "w_in": nrm(ks[2], (DEPTH, D_MODEL, IN_WIDTH), D_MODEL ** -0.5),
        "w_gk2": nrm(ks[3], (DEPTH, 2, GLA_RANK, GLA_K), GLA_RANK ** -0.5),
        "b_gk": nrm(ks[4], (DEPTH, 2, GLA_K), 0.1),
        "gla_norm_w": 1.0 + nrm(ks[5], (DEPTH, GLA_DV), 0.02),
        "lambda_q1": nrm(ks[6], (DEPTH, DIFF_HD), 0.1),
        "lambda_k1": nrm(ks[7], (DEPTH, DIFF_HD), 0.1),
        "lambda_q2": nrm(ks[8], (DEPTH, DIFF_HD), 0.1),
        "lambda_k2": nrm(ks[9], (DEPTH, DIFF_HD), 0.1),
        "diff_subln_w": 1.0 + nrm(ks[10], (DEPTH, 2 * DIFF_HD), 0.02),
        "w_out": nrm(ks[11], (DEPTH, D_MODEL, D_MODEL), D_MODEL ** -0.5),
        "norm_ffn_w": 1.0 + nrm(ks[12], (DEPTH, D_MODEL), 0.02),
        "w_ffn_in": nrm(ks[13], (DEPTH, D_MODEL, 2 * FFN_HIDDEN), D_MODEL ** -0.5),
        "w_ffn_out": nrm(ks[14], (DEPTH, FFN_HIDDEN, D_MODEL), FFN_HIDDEN ** -0.5),
        "norm_final_w": 1.0 + nrm(ks[15], (D_MODEL,), 0.02),
    }


def reference(x, norm_mix_w, w_in, w_gk2, b_gk, gla_norm_w, lambda_q1, lambda_k1, lambda_q2,
              lambda_k2, diff_subln_w, w_out, norm_ffn_w, w_ffn_in, w_ffn_out, norm_final_w):
    h = x
    for layer in range(DEPTH):
        lambda_init = 0.8 - 0.6 * math.exp(-0.3 * layer)
        u = rms_norm(h, norm_mix_w[layer])
        h = h + hybrid_mixer(u, w_in[layer], w_gk2[layer], b_gk[layer], gla_norm_w[layer],
                             lambda_q1[layer], lambda_k1[layer], lambda_q2[layer], lambda_k2[layer],
                             diff_subln_w[layer], w_out[layer], lambda_init)
        h = h + swiglu(rms_norm(h, norm_ffn_w[layer]), w_ffn_in[layer], w_ffn_out[layer])
    return rms_norm(h, norm_final_w)
```

```python
import functools
import math

import jax
import jax.numpy as jnp
from jax import lax
from jax.experimental import pallas as pl
from jax.experimental.pallas import tpu as pltpu

F32 = jnp.float32
BF16 = jnp.bfloat16

D_MODEL = 1024
GLA_HEADS = 4
GLA_K = 512
GLA_DK = 128
GLA_DV = 256
GLA_RANK = 16
GLA_GATE_NORM = 16.0
GLA_CHUNK = 64
DIFF_HEADS = 4
DIFF_HD = 128
ROPE_THETA = 10000.0
FFN_HIDDEN = 2816
NORM_EPS = 1e-6
SUBLN_EPS = 1e-5
LAMBDA_INIT = 0.8 - 0.6 * math.exp(-0.3 * 0)

LANES = 128
PROJ_TILE = 1024
GLA_TILE = 256
DIFF_TQ = 512
TAIL_TM = 512
FFN_CHUNK = 256
VMEM_LIMIT = 56 * 1024 * 1024

_T_QK, _T_GV, _T_GR, _T_DQ, _T_DK, _T_DV, _T_GA, _T_GB = range(8)

NT_DIMS = (((1,), (1,)), ((), ()))
TN_DIMS = (((0,), (0,)), ((), ()))


def _rms(x, w, eps):
    ms = jnp.mean(x * x, axis=-1, keepdims=True)
    return x * lax.rsqrt(ms + eps) * w


def _sigmoid(x):
    return 1.0 / (1.0 + jnp.exp(-x))


def _inproj_kernel(x_ref, nw_ref, w_ref, wglr_ref, cos_ref, sin_ref, out_ref, glr_ref, u_scr):
    j = pl.program_id(1)

    @pl.when(j == 0)
    def _():
        u = _rms(x_ref[...], nw_ref[...], NORM_EPS).astype(BF16)
        u_scr[...] = u
        glr_ref[...] = jnp.dot(u, wglr_ref[...], preferred_element_type=F32)

    acc = jnp.dot(u_scr[...], w_ref[...], preferred_element_type=F32)

    def rope_store(scale):
        cos = cos_ref[...]
        sin = sin_ref[...]
        for g in range(PROJ_TILE // LANES):
            xg = acc[:, g * LANES:(g + 1) * LANES]
            y = xg * cos + pltpu.roll(xg, LANES // 2, axis=1) * sin
            if scale != 1.0:
                y = y * scale
            out_ref[:, g * LANES:(g + 1) * LANES] = y.astype(BF16)

    @pl.when(j == _T_QK)
    def _():
        out_ref[:, :GLA_K] = (acc[:, :GLA_K] * (GLA_DK ** -0.5)).astype(BF16)
        out_ref[:, GLA_K:] = acc[:, GLA_K:].astype(BF16)

    @pl.when(j == _T_DQ)
    def _():
        rope_store((DIFF_HD ** -0.5) * math.log2(math.e))

    @pl.when(j == _T_DK)
    def _():
        rope_store(1.0)

    @pl.when((j != _T_QK) & (j != _T_DQ) & (j != _T_DK))
    def _():
        out_ref[...] = acc.astype(BF16)


def _inproj(x2, nw, w_main, w_glr, cos, sin, seq):
    t = x2.shape[0]
    n_col = w_main.shape[1] // PROJ_TILE
    pos_blocks = seq // PROJ_TILE
    return pl.pallas_call(
        _inproj_kernel,
        grid=(t // PROJ_TILE, n_col),
        in_specs=[
            pl.BlockSpec((PROJ_TILE, D_MODEL), lambda i, j: (i, 0)),
            pl.BlockSpec((1, D_MODEL), lambda i, j: (0, 0)),
            pl.BlockSpec((D_MODEL, PROJ_TILE), lambda i, j: (0, j)),
            pl.BlockSpec((D_MODEL, LANES), lambda i, j: (0, 0)),
            pl.BlockSpec((PROJ_TILE, LANES), lambda i, j: (i % pos_blocks, 0)),
            pl.BlockSpec((PROJ_TILE, LANES), lambda i, j: (i % pos_blocks, 0)),
        ],
        out_specs=[
            pl.BlockSpec((PROJ_TILE, PROJ_TILE), lambda i, j: (i, j)),
            pl.BlockSpec((PROJ_TILE, LANES), lambda i, j: (i, 0)),
        ],
        out_shape=[
            jax.ShapeDtypeStruct((t, w_main.shape[1]), BF16),
            jax.ShapeDtypeStruct((t, LANES), F32),
        ],
        scratch_shapes=[pltpu.VMEM((PROJ_TILE, D_MODEL), BF16)],
        compiler_params=pltpu.CompilerParams(
            dimension_semantics=("parallel", "arbitrary"), vmem_limit_bytes=VMEM_LIMIT),
        name="inproj",
    )(x2, nw, w_main, w_glr, cos, sin)


def _gla_kernel(q_ref, k_ref, v_ref, gr_ref, ga_ref, glr_ref, wgk_ref, bgk_ref, nw_ref,
                tri_ref, blk_ref, out_ref, qt_scr, ks_scr, dec_scr, o_scr, st_scr):
    seq = q_ref.shape[0]
    n_tiles = seq // GLA_TILE
    n_chunks = seq // GLA_CHUNK
    per_tile = GLA_TILE // GLA_CHUNK

    def intra(r, carry):
        r0 = pl.multiple_of(r * GLA_TILE, GLA_TILE)
        rows = pl.ds(r0, GLA_TILE)
        glr = glr_ref[rows, :].astype(BF16)
        q = q_ref[rows, :].astype(F32)
        k = k_ref[rows, :].astype(F32)
        v = v_ref[rows, :]
        blk = blk_ref[...]
        o_acc = jnp.zeros((GLA_TILE, GLA_DV), F32)
        for d in range(2):
            logits = jnp.dot(glr, wgk_ref[d], preferred_element_type=F32) + bgk_ref[d]
            g = (jnp.minimum(logits, 0.0) - jnp.log1p(jnp.exp(-jnp.abs(logits)))) * (1.0 / GLA_GATE_NORM)
            hi = g.astype(BF16)
            r1 = g - hi.astype(F32)
            mid = r1.astype(BF16)
            lo = (r1 - mid.astype(F32)).astype(BF16)
            g3 = jnp.concatenate([hi, mid, lo], axis=1)
            tri = tri_ref[d]
            c3 = jnp.dot(tri, g3, preferred_element_type=F32)
            t3 = jnp.dot(blk, g3, preferred_element_type=F32)
            bcum = c3[:, :LANES] + c3[:, LANES:2 * LANES] + c3[:, 2 * LANES:]
            tot = t3[:, :LANES] + t3[:, LANES:2 * LANES] + t3[:, 2 * LANES:]
            qt = (q * jnp.exp(bcum)).astype(BF16)
            kt = (k * jnp.exp(-bcum)).astype(BF16)
            ks = (k * jnp.exp(tot - bcum)).astype(BF16)
            a = lax.dot_general(qt, kt, NT_DIMS, preferred_element_type=F32)
            a = jnp.where(tri > 0, a, 0.0).astype(BF16)
            o_acc = o_acc + jnp.dot(a, v, preferred_element_type=F32)
            qt_scr[d, rows, :] = qt
            ks_scr[d, rows, :] = ks
            dec = jnp.exp(tot)
            for c in range(per_tile):
                dec_scr[d, pl.ds(r * per_tile + c, 1), :] = dec[c * GLA_CHUNK:c * GLA_CHUNK + 1, :]
        o_scr[rows, :] = o_acc
        return carry

    lax.fori_loop(0, n_tiles, intra, 0)

    st_scr[...] = jnp.zeros(st_scr.shape, F32)

    def inter(n, carry):
        for d in range(2):
            c = n if d == 0 else n_chunks - 1 - n
            r0 = pl.multiple_of(c * GLA_CHUNK, GLA_CHUNK)
            rows = pl.ds(r0, GLA_CHUNK)
            qt = qt_scr[d, rows, :]
            ks = ks_scr[d, rows, :]
            v = v_ref[rows, :]
            st = st_scr[d]
            o_scr[rows, :] += lax.dot_general(qt, st.astype(BF16), NT_DIMS, preferred_element_type=F32)
            kv_t = lax.dot_general(v, ks, TN_DIMS, preferred_element_type=F32)
            st_scr[d] = st * dec_scr[d, pl.ds(c, 1), :] + kv_t
        return carry

    lax.fori_loop(0, n_chunks, inter, 0)

    def finish(r, carry):
        r0 = pl.multiple_of(r * GLA_TILE, GLA_TILE)
        rows = pl.ds(r0, GLA_TILE)
        y = _rms(o_scr[rows, :], nw_ref[...], NORM_EPS)
        gr = gr_ref[rows, :].astype(F32)
        ga = ga_ref[rows, :].astype(F32)
        out_ref[rows, :] = (_sigmoid(ga) * (y * (gr * _sigmoid(gr)))).astype(BF16)
        return carry

    lax.fori_loop(0, n_tiles, finish, 0)


def _gla(proj, glr, wgk, bgk, nw, tri, blk, batch, seq):
    t = proj.shape[0]
    kb = GLA_K // GLA_DK
    vb = PROJ_TILE // GLA_DV
    return pl.pallas_call(
        _gla_kernel,
        grid=(batch, GLA_HEADS),
        in_specs=[
            pl.BlockSpec((seq, GLA_DK), lambda b, h: (b, h)),
            pl.BlockSpec((seq, GLA_DK), lambda b, h: (b, kb + h)),
            pl.BlockSpec((seq, GLA_DV), lambda b, h: (b, _T_GV * vb + h)),
            pl.BlockSpec((seq, GLA_DV), lambda b, h: (b, _T_GR * vb + h)),
            pl.BlockSpec((seq, GLA_DV), lambda b, h: (b, _T_GA * vb + h)),
            pl.BlockSpec((seq, LANES), lambda b, h: (b, 0)),
            pl.BlockSpec((2, LANES, GLA_DK), lambda b, h: (0, 0, h)),
            pl.BlockSpec((2, 1, GLA_DK), lambda b, h: (0, 0, h)),
            pl.BlockSpec((1, GLA_DV), lambda b, h: (0, 0)),
            pl.BlockSpec((2, GLA_TILE, GLA_TILE), lambda b, h: (0, 0, 0)),
            pl.BlockSpec((GLA_TILE, GLA_TILE), lambda b, h: (0, 0)),
        ],
        out_specs=pl.BlockSpec((seq, GLA_DV), lambda b, h: (b, h)),
        out_shape=jax.ShapeDtypeStruct((t, GLA_HEADS * GLA_DV), BF16),
        scratch_shapes=[
            pltpu.VMEM((2, seq, GLA_DK), BF16),
            pltpu.VMEM((2, seq, GLA_DK), BF16),
            pltpu.VMEM((2, seq // GLA_CHUNK, GLA_DK), F32),
            pltpu.VMEM((seq, GLA_DV), F32),
            pltpu.VMEM((2, GLA_DV, GLA_DK), F32),
        ],
        compiler_params=pltpu.CompilerParams(
            dimension_semantics=("parallel", "parallel"), vmem_limit_bytes=VMEM_LIMIT),
        name="gla",
    )(proj, proj, proj, proj, proj, glr, wgk, bgk, nw, tri, blk)


def _diff_kernel(q_ref, k_ref, v_ref, gb_ref, za_ref, lam_ref, nw_ref, out_ref):
    lv = lam_ref[...]
    lam = (jnp.exp(jnp.sum(lv[0:1] * lv[1:2], keepdims=True))
           - jnp.exp(jnp.sum(lv[2:3] * lv[3:4], keepdims=True)) + LAMBDA_INIT)
    v = v_ref[...]
    outs = []
    for c in range(2):
        qc = q_ref[:, c * DIFF_HD:(c + 1) * DIFF_HD]
        kc = k_ref[:, c * DIFF_HD:(c + 1) * DIFF_HD]
        s = lax.dot_general(qc, kc, NT_DIMS, preferred_element_type=F32)
        m = jnp.max(s, axis=-1, keepdims=True)
        p = jnp.exp2(s - m)
        l = jnp.sum(p, axis=-1, keepdims=True)
        acc = jnp.dot(p.astype(BF16), v, preferred_element_type=F32)
        outs.append(acc / l)
    o = outs[0] - lam * outs[1]
    y = _rms(o, nw_ref[...], SUBLN_EPS) * (1.0 - LAMBDA_INIT)
    gb = gb_ref[...].astype(F32)
    out_ref[...] = (za_ref[...].astype(F32) + _sigmoid(gb) * y).astype(BF16)


def _diff(proj, za, lam4, nw, batch, seq):
    t = proj.shape[0]
    hw = 2 * DIFF_HD
    vb = PROJ_TILE // hw
    nq = seq // DIFF_TQ
    return pl.pallas_call(
        _diff_kernel,
        grid=(batch, DIFF_HEADS, nq),
        in_specs=[
            pl.BlockSpec((DIFF_TQ, hw), lambda b, h, i: (b * nq + i, _T_DQ * vb + h)),
            pl.BlockSpec((seq, hw), lambda b, h, i: (b, _T_DK * vb + h)),
            pl.BlockSpec((seq, hw), lambda b, h, i: (b, _T_DV * vb + h)),
            pl.BlockSpec((DIFF_TQ, hw), lambda b, h, i: (b * nq + i, _T_GB * vb + h)),
            pl.BlockSpec((DIFF_TQ, hw), lambda b, h, i: (b * nq + i, h)),
            pl.BlockSpec((4, DIFF_HD), lambda b, h, i: (0, 0)),
            pl.BlockSpec((1, hw), lambda b, h, i: (0, 0)),
        ],
        out_specs=pl.BlockSpec((DIFF_TQ, hw), lambda b, h, i: (b * nq + i, h)),
        out_shape=jax.ShapeDtypeStruct((t, DIFF_HEADS * hw), BF16),
        compiler_params=pltpu.CompilerParams(
            dimension_semantics=("parallel", "parallel", "arbitrary"), vmem_limit_bytes=VMEM_LIMIT),
        name="diffattn",
    )(proj, proj, proj, proj, za, lam4, nw)


def _tail_kernel(m_ref, x_ref, wo_ref, nfw_ref, wg_ref, wu_ref, wd_ref, nlw_ref, out_ref):
    h1 = x_ref[...] + jnp.dot(m_ref[...], wo_ref[...], preferred_element_type=F32)
    n = _rms(h1, nfw_ref[...], NORM_EPS).astype(BF16)
    acc = h1
    for c in range(wg_ref.shape[0]):
        g = jnp.dot(n, wg_ref[c], preferred_element_type=F32)
        u = jnp.dot(n, wu_ref[c], preferred_element_type=F32)
        hd = ((g * _sigmoid(g)) * u).astype(BF16)
        acc = acc + jnp.dot(hd, wd_ref[c], preferred_element_type=F32)
    out_ref[...] = _rms(acc, nlw_ref[...], NORM_EPS)


def _tail(merged, x2, wo, nfw, wg, wu, wd, nlw):
    t = x2.shape[0]
    nc = wg.shape[0]
    resident = pl.Buffered(1)
    return pl.pallas_call(
        _tail_kernel,
        grid=(t // TAIL_TM,),
        in_specs=[
            pl.BlockSpec((TAIL_TM, D_MODEL), lambda i: (i, 0)),
            pl.BlockSpec((TAIL_TM, D_MODEL), lambda i: (i, 0)),
            pl.BlockSpec((D_MODEL, D_MODEL), lambda i: (0, 0), pipeline_mode=resident),
            pl.BlockSpec((1, D_MODEL), lambda i: (0, 0)),
            pl.BlockSpec((nc, D_MODEL, FFN_CHUNK), lambda i: (0, 0, 0), pipeline_mode=resident),
            pl.BlockSpec((nc, D_MODEL, FFN_CHUNK), lambda i: (0, 0, 0), pipeline_mode=resident),
            pl.BlockSpec((nc, FFN_CHUNK, D_MODEL), lambda i: (0, 0, 0), pipeline_mode=resident),
            pl.BlockSpec((1, D_MODEL), lambda i: (0, 0)),
        ],
        out_specs=pl.BlockSpec((TAIL_TM, D_MODEL), lambda i: (i, 0)),
        out_shape=jax.ShapeDtypeStruct((t, D_MODEL), F32),
        compiler_params=pltpu.CompilerParams(
            dimension_semantics=("parallel",), vmem_limit_bytes=VMEM_LIMIT),
        name="tail",
    )(merged, x2, wo, nfw, wg, wu, wd, nlw)


def _rope_tables(seq):
    inv_freq = 1.0 / (ROPE_THETA ** (jnp.arange(0, DIFF_HD, 2, dtype=F32) / DIFF_HD))
    freqs = jnp.arange(seq, dtype=F32)[:, None] * inv_freq[None, :]
    emb = jnp.concatenate([freqs, freqs], axis=-1)
    half_sign = jnp.concatenate([-jnp.ones((DIFF_HD // 2,), F32), jnp.ones((DIFF_HD // 2,), F32)])
    return jnp.cos(emb), jnp.sin(emb) * half_sign[None, :]


def _chunk_masks():
    r = jnp.arange(GLA_TILE)
    same = (r[:, None] // GLA_CHUNK) == (r[None, :] // GLA_CHUNK)
    lower = same & (r[None, :] <= r[:, None])
    upper = same & (r[None, :] >= r[:, None])
    return jnp.stack([lower, upper]).astype(BF16), same.astype(BF16)


def kernel(x, norm_mix_w, w_in, w_gk2, b_gk, gla_norm_w, lambda_q1, lambda_k1, lambda_q2,
           lambda_k2, diff_subln_w, w_out, norm_ffn_w, w_ffn_in, w_ffn_out, norm_final_w):
    batch, seq, d = x.shape
    assert d == D_MODEL and seq % PROJ_TILE == 0 and w_in.shape[0] == 1
    x2 = x.reshape(batch * seq, d)

    w = w_in[0]
    o_glr = 2 * GLA_K + 2 * D_MODEL
    w_main = jnp.concatenate([w[:, :o_glr], w[:, o_glr + 2 * GLA_RANK:]], axis=1).astype(BF16)
    w_glr = jnp.pad(w[:, o_glr:o_glr + 2 * GLA_RANK], ((0, 0), (0, LANES - 2 * GLA_RANK))).astype(BF16)
    wgk = jnp.zeros((2, LANES, GLA_K), F32)
    wgk = wgk.at[0, :GLA_RANK].set(w_gk2[0, 0]).at[1, GLA_RANK:2 * GLA_RANK].set(w_gk2[0, 1]).astype(BF16)
    bgk = b_gk[0].reshape(2, 1, GLA_K)
    cos, sin = _rope_tables(seq)
    tri, blk = _chunk_masks()
    lam4 = jnp.stack([lambda_q1[0], lambda_k1[0], lambda_q2[0], lambda_k2[0]])
    nc = FFN_HIDDEN // FFN_CHUNK
    wf = w_ffn_in[0].astype(BF16)
    wg = wf[:, :FFN_HIDDEN].reshape(d, nc, FFN_CHUNK).transpose(1, 0, 2)
    wu = wf[:, FFN_HIDDEN:].reshape(d, nc, FFN_CHUNK).transpose(1, 0, 2)
    wd = w_ffn_out[0].astype(BF16).reshape(nc, FFN_CHUNK, d)

    proj, glr = _inproj(x2, norm_mix_w, w_main, w_glr, cos, sin, seq)
    za = _gla(proj, glr, wgk, bgk, gla_norm_w, tri, blk, batch, seq)
    merged = _diff(proj, za, lam4, diff_subln_w, batch, seq)
    out = _tail(merged, x2, w_out[0].astype(BF16), norm_ffn_w, wg, wu, wd,
                norm_final_w.reshape(1, d))
    return out.reshape(batch, seq, d)
```

```python
import functools
import math

import jax
import jax.numpy as jnp
from jax import lax
from jax.experimental import pallas as pl
from jax.experimental.pallas import tpu as pltpu

F32 = jnp.float32
BF16 = jnp.bfloat16

D_MODEL = 1024
GLA_HEADS = 4
GLA_K = 512
GLA_DK = 128
GLA_DV = 256
GLA_RANK = 16
GLA_GATE_NORM = 16.0
GLA_CHUNK = 64
DIFF_HEADS = 4
DIFF_HD = 128
ROPE_THETA = 10000.0
FFN_HIDDEN = 2816
NORM_EPS = 1e-6
SUBLN_EPS = 1e-5
LAMBDA_INIT = 0.8 - 0.6 * math.exp(-0.3 * 0)

LANES = 128
PROJ_TILE = 1024
GLA_TILE = 256
DIFF_TQ = 512
DIFF_SUB = DIFF_TQ // 2
TAIL_TM = 512
FFN_CHUNK = 256
VMEM_LIMIT = 56 * 1024 * 1024

_T_QK, _T_GV, _T_GR, _T_DQ, _T_DK, _T_DV, _T_GA, _T_GB = range(8)

NT_DIMS = (((1,), (1,)), ((), ()))
TN_DIMS = (((0,), (0,)), ((), ()))


def _rms(x, w, eps):
    ms = jnp.mean(x * x, axis=-1, keepdims=True)
    return x * lax.rsqrt(ms + eps) * w


def _sigmoid(x):
    return 0.5 * jnp.tanh(0.5 * x) + 0.5


def _inproj_kernel(x_ref, nw_ref, w_ref, wglr_ref, wgk_ref, bgk_ref, cos_ref, sin_ref,
                   out_ref, g_ref, u_scr):
    j = pl.program_id(1)

    @pl.when(j == 0)
    def _():
        u = _rms(x_ref[...], nw_ref[...], NORM_EPS).astype(BF16)
        u_scr[...] = u
        glr = jnp.dot(u, wglr_ref[...], preferred_element_type=F32).astype(BF16)
        logits = jnp.dot(glr, wgk_ref[...], preferred_element_type=F32) + bgk_ref[...]
        g_ref[...] = ((jnp.minimum(logits, 0.0) - jnp.log(1.0 + jnp.exp(-jnp.abs(logits))))
                      * (1.0 / GLA_GATE_NORM))

    acc = jnp.dot(u_scr[...], w_ref[...], preferred_element_type=F32)

    def rope_store(scale):
        cos = cos_ref[...]
        sin = sin_ref[...]
        for g in range(PROJ_TILE // LANES):
            xg = acc[:, g * LANES:(g + 1) * LANES]
            y = xg * cos + pltpu.roll(xg, LANES // 2, axis=1) * sin
            if scale != 1.0:
                y = y * scale
            out_ref[:, g * LANES:(g + 1) * LANES] = y.astype(BF16)

    @pl.when(j == _T_QK)
    def _():
        out_ref[:, :GLA_K] = (acc[:, :GLA_K] * (GLA_DK ** -0.5)).astype(BF16)
        out_ref[:, GLA_K:] = acc[:, GLA_K:].astype(BF16)

    @pl.when(j == _T_DQ)
    def _():
        rope_store((DIFF_HD ** -0.5) * math.log2(math.e))

    @pl.when(j == _T_DK)
    def _():
        rope_store(1.0)

    @pl.when((j != _T_QK) & (j != _T_DQ) & (j != _T_DK))
    def _():
        out_ref[...] = acc.astype(BF16)


def _inproj(x2, nw, w_main, w_glr, wgk, bgk, cos, sin, seq):
    t = x2.shape[0]
    n_col = w_main.shape[1] // PROJ_TILE
    pos_blocks = seq // PROJ_TILE
    return pl.pallas_call(
        _inproj_kernel,
        grid=(t // PROJ_TILE, n_col),
        in_specs=[
            pl.BlockSpec((PROJ_TILE, D_MODEL), lambda i, j: (i, 0)),
            pl.BlockSpec((1, D_MODEL), lambda i, j: (0, 0)),
            pl.BlockSpec((D_MODEL, PROJ_TILE), lambda i, j: (0, j)),
            pl.BlockSpec((D_MODEL, LANES), lambda i, j: (0, 0)),
            pl.BlockSpec((LANES, 2 * GLA_K), lambda i, j: (0, 0)),
            pl.BlockSpec((1, 2 * GLA_K), lambda i, j: (0, 0)),
            pl.BlockSpec((PROJ_TILE, LANES), lambda i, j: (i % pos_blocks, 0)),
            pl.BlockSpec((PROJ_TILE, LANES), lambda i, j: (i % pos_blocks, 0)),
        ],
        out_specs=[
            pl.BlockSpec((PROJ_TILE, PROJ_TILE), lambda i, j: (i, j)),
            pl.BlockSpec((PROJ_TILE, 2 * GLA_K), lambda i, j: (i, 0)),
        ],
        out_shape=[
            jax.ShapeDtypeStruct((t, w_main.shape[1]), BF16),
            jax.ShapeDtypeStruct((t, 2 * GLA_K), F32),
        ],
        scratch_shapes=[pltpu.VMEM((PROJ_TILE, D_MODEL), BF16)],
        compiler_params=pltpu.CompilerParams(
            dimension_semantics=("parallel", "arbitrary"), vmem_limit_bytes=VMEM_LIMIT),
        name="inproj",
    )(x2, nw, w_main, w_glr, wgk, bgk, cos, sin)


def _gla_kernel(q_ref, k_ref, v_ref, gr_ref, ga_ref, gf_ref, gb_ref, nw_ref, tri_ref, out_ref,
                qt_scr, kt_scr, ks_scr, dec_scr, of_scr, ob_scr, st_scr):
    seq = q_ref.shape[0]
    n_tiles = seq // GLA_TILE
    per_tile = GLA_TILE // GLA_CHUNK
    g_refs = (gf_ref, gb_ref)
    o_scrs = (of_scr, ob_scr)

    def decays(r, carry):
        r0 = pl.multiple_of(r * GLA_TILE, GLA_TILE)
        rows = pl.ds(r0, GLA_TILE)
        q = q_ref[rows, :].astype(F32)
        k = k_ref[rows, :].astype(F32)
        for d in range(2):
            g = g_refs[d][rows, :]
            hi = g.astype(BF16)
            lo = (g - hi.astype(F32)).astype(BF16)
            c2 = jnp.dot(tri_ref[d], jnp.concatenate([hi, lo], axis=1), preferred_element_type=F32)
            bcum = c2[:, :LANES] + c2[:, LANES:]
            edge = GLA_CHUNK - 1 if d == 0 else 0
            tots = [bcum[c * GLA_CHUNK + edge:c * GLA_CHUNK + edge + 1, :] for c in range(per_tile)]
            tot = jnp.concatenate([jnp.broadcast_to(tc, (GLA_CHUNK, GLA_DK)) for tc in tots], axis=0)
            qt_scr[d, rows, :] = (q * jnp.exp(bcum)).astype(BF16)
            kt_scr[d, rows, :] = (k * jnp.exp(-bcum)).astype(BF16)
            ks_scr[d, rows, :] = (k * jnp.exp(tot - bcum)).astype(BF16)
            for c in range(per_tile):
                dec_scr[d, r, pl.ds(c, 1), :] = jnp.exp(tots[c])
        return carry

    lax.fori_loop(0, n_tiles, decays, 0, unroll=2)

    st_scr[...] = jnp.zeros(st_scr.shape, F32)

    def scan(n, carry):
        for d in range(2):
            t = n if d == 0 else n_tiles - 1 - n
            base = pl.multiple_of(t * GLA_TILE, GLA_TILE)
            rows = pl.ds(base, GLA_TILE)
            qt = qt_scr[d, rows, :]
            ks = ks_scr[d, rows, :]
            v = v_ref[rows, :]
            a = lax.dot_general(qt, kt_scr[d, rows, :], NT_DIMS, preferred_element_type=F32)
            a = jnp.where(tri_ref[d] > 0, a, 0.0).astype(BF16)
            o_intra = jnp.dot(a, v, preferred_element_type=F32)
            st = st_scr[d]
            pieces = [None] * per_tile
            for c in (range(per_tile) if d == 0 else range(per_tile - 1, -1, -1)):
                sl = slice(c * GLA_CHUNK, (c + 1) * GLA_CHUNK)
                pieces[c] = o_intra[sl] + lax.dot_general(
                    qt[sl], st.astype(BF16), NT_DIMS, preferred_element_type=F32)
                kv_t = lax.dot_general(v[sl], ks[sl], TN_DIMS, preferred_element_type=F32)
                st = st * dec_scr[d, t, pl.ds(c, 1), :] + kv_t
            o_scrs[d][rows, :] = jnp.concatenate(pieces, axis=0)
            st_scr[d] = st
        return carry

    lax.fori_loop(0, n_tiles, scan, 0)

    def finish(r, carry):
        r0 = pl.multiple_of(r * GLA_TILE, GLA_TILE)
        rows = pl.ds(r0, GLA_TILE)
        y = _rms(of_scr[rows, :] + ob_scr[rows, :], nw_ref[...], NORM_EPS)
        gr = gr_ref[rows, :].astype(F32)
        ga = ga_ref[rows, :].astype(F32)
        out_ref[rows, :] = (_sigmoid(ga) * (y * (gr * _sigmoid(gr)))).astype(BF16)
        return carry

    lax.fori_loop(0, n_tiles, finish, 0)


def _gla(proj, g, nw, tri, batch, seq):
    t = proj.shape[0]
    kb = GLA_K // GLA_DK
    vb = PROJ_TILE // GLA_DV
    return pl.pallas_call(
        _gla_kernel,
        grid=(batch, GLA_HEADS),
        in_specs=[
            pl.BlockSpec((seq, GLA_DK), lambda b, h: (b, h)),
            pl.BlockSpec((seq, GLA_DK), lambda b, h: (b, kb + h)),
            pl.BlockSpec((seq, GLA_DV), lambda b, h: (b, _T_GV * vb + h)),
            pl.BlockSpec((seq, GLA_DV), lambda b, h: (b, _T_GR * vb + h)),
            pl.BlockSpec((seq, GLA_DV), lambda b, h: (b, _T_GA * vb + h)),
            pl.BlockSpec((seq, GLA_DK), lambda b, h: (b, h)),
            pl.BlockSpec((seq, GLA_DK), lambda b, h: (b, kb + h)),
            pl.BlockSpec((1, GLA_DV), lambda b, h: (0, 0)),
            pl.BlockSpec((2, GLA_TILE, GLA_TILE), lambda b, h: (0, 0, 0)),
        ],
        out_specs=pl.BlockSpec((seq, GLA_DV), lambda b, h: (b, h)),
        out_shape=jax.ShapeDtypeStruct((t, GLA_HEADS * GLA_DV), BF16),
        scratch_shapes=[
            pltpu.VMEM((2, seq, GLA_DK), BF16),
            pltpu.VMEM((2, seq, GLA_DK), BF16),
            pltpu.VMEM((2, seq, GLA_DK), BF16),
            pltpu.VMEM((2, seq // GLA_TILE, GLA_TILE // GLA_CHUNK, GLA_DK), F32),
            pltpu.VMEM((seq, GLA_DV), F32),
            pltpu.VMEM((seq, GLA_DV), F32),
            pltpu.VMEM((2, GLA_DV, GLA_DK), F32),
        ],
        compiler_params=pltpu.CompilerParams(
            dimension_semantics=("parallel", "parallel"), vmem_limit_bytes=VMEM_LIMIT),
        name="gla",
    )(proj, proj, proj, proj, proj, g, g, nw, tri)


def _diff_kernel(q_ref, k_ref, v_ref, gb_ref, za_ref, lam_ref, nw_ref, out_ref, kt_scr):
    @pl.when(pl.program_id(2) == 0)
    def _():
        kt_scr[...] = k_ref[...].T

    lv = lam_ref[...]
    lam = (jnp.exp(jnp.sum(lv[0:1] * lv[1:2], keepdims=True))
           - jnp.exp(jnp.sum(lv[2:3] * lv[3:4], keepdims=True)) + LAMBDA_INIT)
    v = v_ref[...]
    outs = []
    for c in range(2):
        qc = q_ref[:, c * DIFF_HD:(c + 1) * DIFF_HD]
        s = jnp.dot(qc, kt_scr[c * DIFF_HD:(c + 1) * DIFF_HD, :], preferred_element_type=F32)
        m = jnp.max(s, axis=-1, keepdims=True)
        p = jnp.exp2(s - m)
        l = jnp.sum(p, axis=-1, keepdims=True)
        p = p.astype(BF16)
        acc = jnp.concatenate(
            [jnp.dot(p[:DIFF_SUB], v, preferred_element_type=F32),
             jnp.dot(p[DIFF_SUB:], v, preferred_element_type=F32)], axis=0)
        outs.append(acc / l)
    o = outs[0] - lam * outs[1]
    y = _rms(o, nw_ref[...], SUBLN_EPS) * (1.0 - LAMBDA_INIT)
    gb = gb_ref[...].astype(F32)
    out_ref[...] = (za_ref[...].astype(F32) + _sigmoid(gb) * y).astype(BF16)


def _diff(proj, za, lam4, nw, batch, seq):
    t = proj.shape[0]
    hw = 2 * DIFF_HD
    vb = PROJ_TILE // hw
    nq = seq // DIFF_TQ
    return pl.pallas_call(
        _diff_kernel,
        grid=(batch, DIFF_HEADS, nq),
        in_specs=[
            pl.BlockSpec((DIFF_TQ, hw), lambda b, h, i: (b * nq + i, _T_DQ * vb + h)),
            pl.BlockSpec((seq, hw), lambda b, h, i: (b, _T_DK * vb + h)),
            pl.BlockSpec((seq, hw), lambda b, h, i: (b, _T_DV * vb + h)),
            pl.BlockSpec((DIFF_TQ, hw), lambda b, h, i: (b * nq + i, _T_GB * vb + h)),
            pl.BlockSpec((DIFF_TQ, hw), lambda b, h, i: (b * nq + i, h)),
            pl.BlockSpec((4, DIFF_HD), lambda b, h, i: (0, 0)),
            pl.BlockSpec((1, hw), lambda b, h, i: (0, 0)),
        ],
        out_specs=pl.BlockSpec((DIFF_TQ, hw), lambda b, h, i: (b * nq + i, h)),
        out_shape=jax.ShapeDtypeStruct((t, DIFF_HEADS * hw), BF16),
        scratch_shapes=[pltpu.VMEM((hw, seq), BF16)],
        compiler_params=pltpu.CompilerParams(
            dimension_semantics=("parallel", "parallel", "arbitrary"), vmem_limit_bytes=VMEM_LIMIT),
        name="diffattn",
    )(proj, proj, proj, proj, za, lam4, nw)


def _tail_kernel(m_ref, x_ref, wo_ref, nfw_ref, wg_ref, wu_ref, wd_ref, nlw_ref, out_ref):
    h1 = x_ref[...] + jnp.dot(m_ref[...], wo_ref[...], preferred_element_type=F32)
    n = _rms(h1, nfw_ref[...], NORM_EPS).astype(BF16)
    acc = h1
    for c in range(wg_ref.shape[0]):
        g = jnp.dot(n, wg_ref[c], preferred_element_type=F32)
        u = jnp.dot(n, wu_ref[c], preferred_element_type=F32)
        hd = ((g * _sigmoid(g)) * u).astype(BF16)
        acc = acc + jnp.dot(hd, wd_ref[c], preferred_element_type=F32)
    out_ref[...] = _rms(acc, nlw_ref[...], NORM_EPS)


def _tail(merged, x2, wo, nfw, wg, wu, wd, nlw):
    t = x2.shape[0]
    nc = wg.shape[0]
    resident = pl.Buffered(1)
    return pl.pallas_call(
        _tail_kernel,
        grid=(t // TAIL_TM,),
        in_specs=[
            pl.BlockSpec((TAIL_TM, D_MODEL), lambda i: (i, 0)),
            pl.BlockSpec((TAIL_TM, D_MODEL), lambda i: (i, 0)),
            pl.BlockSpec((D_MODEL, D_MODEL), lambda i: (0, 0), pipeline_mode=resident),
            pl.BlockSpec((1, D_MODEL), lambda i: (0, 0)),
            pl.BlockSpec((nc, D_MODEL, FFN_CHUNK), lambda i: (0, 0, 0), pipeline_mode=resident),
            pl.BlockSpec((nc, D_MODEL, FFN_CHUNK), lambda i: (0, 0, 0), pipeline_mode=resident),
            pl.BlockSpec((nc, FFN_CHUNK, D_MODEL), lambda i: (0, 0, 0), pipeline_mode=resident),
            pl.BlockSpec((1, D_MODEL), lambda i: (0, 0)),
        ],
        out_specs=pl.BlockSpec((TAIL_TM, D_MODEL), lambda i: (i, 0)),
        out_shape=jax.ShapeDtypeStruct((t, D_MODEL), F32),
        compiler_params=pltpu.CompilerParams(
            dimension_semantics=("parallel",), vmem_limit_bytes=VMEM_LIMIT),
        name="tail",
    )(merged, x2, wo, nfw, wg, wu, wd, nlw)


def _rope_tables(seq):
    inv_freq = 1.0 / (ROPE_THETA ** (jnp.arange(0, DIFF_HD, 2, dtype=F32) / DIFF_HD))
    freqs = jnp.arange(seq, dtype=F32)[:, None] * inv_freq[None, :]
    emb = jnp.concatenate([freqs, freqs], axis=-1)
    half_sign = jnp.concatenate([-jnp.ones((DIFF_HD // 2,), F32), jnp.ones((DIFF_HD // 2,), F32)])
    return jnp.cos(emb), jnp.sin(emb) * half_sign[None, :]


def _chunk_masks():
    r = jnp.arange(GLA_TILE)
    same = (r[:, None] // GLA_CHUNK) == (r[None, :] // GLA_CHUNK)
    lower = same & (r[None, :] <= r[:, None])
    upper = same & (r[None, :] >= r[:, None])
    return jnp.stack([lower, upper]).astype(BF16)


def kernel(x, norm_mix_w, w_in, w_gk2, b_gk, gla_norm_w, lambda_q1, lambda_k1, lambda_q2,
           lambda_k2, diff_subln_w, w_out, norm_ffn_w, w_ffn_in, w_ffn_out, norm_final_w):
    batch, seq, d = x.shape
    assert d == D_MODEL and seq % PROJ_TILE == 0 and w_in.shape[0] == 1
    x2 = x.reshape(batch * seq, d)

    w = w_in[0]
    o_glr = 2 * GLA_K + 2 * D_MODEL
    w_main = jnp.concatenate([w[:, :o_glr], w[:, o_glr + 2 * GLA_RANK:]], axis=1).astype(BF16)
    w_glr = jnp.pad(w[:, o_glr:o_glr + 2 * GLA_RANK], ((0, 0), (0, LANES - 2 * GLA_RANK))).astype(BF16)
    wgk = jnp.zeros((LANES, 2 * GLA_K), F32)
    wgk = (wgk.at[:GLA_RANK, :GLA_K].set(w_gk2[0, 0])
           .at[GLA_RANK:2 * GLA_RANK, GLA_K:].set(w_gk2[0, 1]).astype(BF16))
    bgk = b_gk[0].reshape(1, 2 * GLA_K)
    cos, sin = _rope_tables(seq)
    tri = _chunk_masks()
    lam4 = jnp.stack([lambda_q1[0], lambda_k1[0], lambda_q2[0], lambda_k2[0]])
    nc = FFN_HIDDEN // FFN_CHUNK
    wf = w_ffn_in[0].astype(BF16)
    wg = wf[:, :FFN_HIDDEN].reshape(d, nc, FFN_CHUNK).transpose(1, 0, 2)
    wu = wf[:, FFN_HIDDEN:].reshape(d, nc, FFN_CHUNK).transpose(1, 0, 2)
    wd = w_ffn_out[0].astype(BF16).reshape(nc, FFN_CHUNK, d)

    proj, g = _inproj(x2, norm_mix_w, w_main, w_glr, wgk, bgk, cos, sin, seq)
    za = _gla(proj, g, gla_norm_w, tri, batch, seq)
    merged = _diff(proj, za, lam4, diff_subln_w, batch, seq)
    out = _tail(merged, x2, w_out[0].astype(BF16), norm_ffn_w, wg, wu, wd,
                norm_final_w.reshape(1, d))
    return out.reshape(batch, seq, d)
```

```python
import functools
import math

import jax
import jax.numpy as jnp
from jax import lax
from jax.experimental import pallas as pl
from jax.experimental.pallas import tpu as pltpu

F32 = jnp.float32
BF16 = jnp.bfloat16

D_MODEL = 1024
GLA_HEADS = 4
GLA_K = 512
GLA_DK = 128
GLA_DV = 256
GLA_RANK = 16
GLA_GATE_NORM = 16.0
GLA_CHUNK = 64
DIFF_HEADS = 4
DIFF_HD = 128
ROPE_THETA = 10000.0
FFN_HIDDEN = 2816
NORM_EPS = 1e-6
SUBLN_EPS = 1e-5
LAMBDA_INIT = 0.8 - 0.6 * math.exp(-0.3 * 0)

LANES = 128
PROJ_TILE = 1024
GLA_TILE = 256
DIFF_TQ = 1024
DIFF_ROWS = 512
DIFF_SUB = DIFF_ROWS // 2
TAIL_TM = 512
FFN_CHUNK = 256
VMEM_LIMIT = 56 * 1024 * 1024

_T_QK, _T_GV, _T_GR, _T_DQ, _T_DK, _T_DV, _T_GA, _T_GB = range(8)

NT_DIMS = (((1,), (1,)), ((), ()))
TN_DIMS = (((0,), (0,)), ((), ()))


def _rms(x, w, eps):
    ms = jnp.mean(x * x, axis=-1, keepdims=True)
    return x * lax.rsqrt(ms + eps) * w


def _sigmoid(x):
    return 0.5 * jnp.tanh(0.5 * x) + 0.5


def _inproj_kernel(x_ref, nw_ref, w_ref, wglr_ref, wgk_ref, bgk_ref, cos_ref, sin_ref,
                   out_ref, g_ref, u_scr):
    j = pl.program_id(1)
    assert _T_QK == 0

    def project():
        return jnp.dot(u_scr[...], w_ref[...], preferred_element_type=F32)

    def rope_store(scale):
        acc = project()
        cos = cos_ref[...]
        sin = sin_ref[...]
        for g in range(PROJ_TILE // LANES):
            xg = acc[:, g * LANES:(g + 1) * LANES]
            y = xg * cos + pltpu.roll(xg, LANES // 2, axis=1) * sin
            if scale != 1.0:
                y = y * scale
            out_ref[:, g * LANES:(g + 1) * LANES] = y.astype(BF16)

    @pl.when(j == _T_QK)
    def _():
        u = _rms(x_ref[...], nw_ref[...], NORM_EPS).astype(BF16)
        u_scr[...] = u
        half = PROJ_TILE // 2
        wglr = wglr_ref[...]
        glr = jnp.concatenate(
            [jnp.dot(u[:half], wglr, preferred_element_type=F32),
             jnp.dot(u[half:], wglr, preferred_element_type=F32)], axis=0).astype(BF16)
        logits = jnp.dot(glr, wgk_ref[...], preferred_element_type=F32) + bgk_ref[...]
        soft = jnp.log2(1.0 + jnp.exp2(jnp.abs(logits) * (-math.log2(math.e))))
        g_ref[...] = (jnp.minimum(logits, 0.0) * (1.0 / GLA_GATE_NORM)
                      - soft * (math.log(2.0) / GLA_GATE_NORM))
        acc = jnp.dot(u, w_ref[...], preferred_element_type=F32)
        out_ref[:, :GLA_K] = (acc[:, :GLA_K] * (GLA_DK ** -0.5)).astype(BF16)
        out_ref[:, GLA_K:] = acc[:, GLA_K:].astype(BF16)

    @pl.when(j == _T_DQ)
    def _():
        rope_store((DIFF_HD ** -0.5) * math.log2(math.e))

    @pl.when(j == _T_DK)
    def _():
        rope_store(1.0)

    @pl.when((j != _T_QK) & (j != _T_DQ) & (j != _T_DK))
    def _():
        out_ref[...] = project().astype(BF16)


def _inproj(x2, nw, w_main, w_glr, wgk, bgk, cos, sin, seq):
    t = x2.shape[0]
    n_col = w_main.shape[1] // PROJ_TILE
    pos_blocks = seq // PROJ_TILE
    return pl.pallas_call(
        _inproj_kernel,
        grid=(t // PROJ_TILE, n_col),
        in_specs=[
            pl.BlockSpec((PROJ_TILE, D_MODEL), lambda i, j: (i, 0)),
            pl.BlockSpec((1, D_MODEL), lambda i, j: (0, 0)),
            pl.BlockSpec((D_MODEL, PROJ_TILE), lambda i, j: (0, j)),
            pl.BlockSpec((D_MODEL, LANES), lambda i, j: (0, 0)),
            pl.BlockSpec((LANES, 2 * GLA_K), lambda i, j: (0, 0)),
            pl.BlockSpec((1, 2 * GLA_K), lambda i, j: (0, 0)),
            pl.BlockSpec((PROJ_TILE, LANES), lambda i, j: (i % pos_blocks, 0)),
            pl.BlockSpec((PROJ_TILE, LANES), lambda i, j: (i % pos_blocks, 0)),
        ],
        out_specs=[
            pl.BlockSpec((PROJ_TILE, PROJ_TILE), lambda i, j: (i, j)),
            pl.BlockSpec((PROJ_TILE, 2 * GLA_K), lambda i, j: (i, 0)),
        ],
        out_shape=[
            jax.ShapeDtypeStruct((t, w_main.shape[1]), BF16),
            jax.ShapeDtypeStruct((t, 2 * GLA_K), F32),
        ],
        scratch_shapes=[pltpu.VMEM((PROJ_TILE, D_MODEL), BF16)],
        compiler_params=pltpu.CompilerParams(
            dimension_semantics=("parallel", "arbitrary"), vmem_limit_bytes=VMEM_LIMIT),
        name="inproj",
    )(x2, nw, w_main, w_glr, wgk, bgk, cos, sin)


def _gla_kernel(q_ref, k_ref, v_ref, gr_ref, ga_ref, gf_ref, gb_ref, nw_ref, tri_ref, out_ref,
                qt_scr, kt_scr, ks_scr, dec_scr, of_scr, ob_scr, st_scr):
    seq = q_ref.shape[0]
    n_tiles = seq // GLA_TILE
    per_tile = GLA_TILE // GLA_CHUNK
    g_refs = (gf_ref, gb_ref)
    o_scrs = (of_scr, ob_scr)

    def decays(r, carry):
        r0 = pl.multiple_of(r * GLA_TILE, GLA_TILE)
        rows = pl.ds(r0, GLA_TILE)
        q = q_ref[rows, :].astype(F32)
        k = k_ref[rows, :].astype(F32)
        for d in range(2):
            g = g_refs[d][rows, :]
            hi = g.astype(BF16)
            lo = (g - hi.astype(F32)).astype(BF16)
            c2 = jnp.dot(tri_ref[d], jnp.concatenate([hi, lo], axis=1), preferred_element_type=F32)
            bcum = c2[:, :LANES] + c2[:, LANES:]
            edge = GLA_CHUNK - 1 if d == 0 else 0
            tots = [bcum[c * GLA_CHUNK + edge:c * GLA_CHUNK + edge + 1, :] for c in range(per_tile)]
            tot = jnp.concatenate([jnp.broadcast_to(tc, (GLA_CHUNK, GLA_DK)) for tc in tots], axis=0)
            qt_scr[d, rows, :] = (q * jnp.exp(bcum)).astype(BF16)
            kt_scr[d, rows, :] = (k * jnp.exp(-bcum)).astype(BF16)
            ks_scr[d, rows, :] = (k * jnp.exp(tot - bcum)).astype(BF16)
            for c in range(per_tile):
                dec_scr[d, r, pl.ds(c, 1), :] = jnp.exp(tots[c])
        return carry

    lax.fori_loop(0, n_tiles, decays, 0, unroll=2)

    st_scr[...] = jnp.zeros(st_scr.shape, F32)

    def scan(n, carry):
        for d in range(2):
            t = n if d == 0 else n_tiles - 1 - n
            base = pl.multiple_of(t * GLA_TILE, GLA_TILE)
            rows = pl.ds(base, GLA_TILE)
            qt = qt_scr[d, rows, :]
            ks = ks_scr[d, rows, :]
            v = v_ref[rows, :]
            a = lax.dot_general(qt, kt_scr[d, rows, :], NT_DIMS, preferred_element_type=F32)
            a = jnp.where(tri_ref[d] > 0, a, 0.0).astype(BF16)
            o_intra = jnp.dot(a, v, preferred_element_type=F32)
            st = st_scr[d]
            pieces = [None] * per_tile
            for c in (range(per_tile) if d == 0 else range(per_tile - 1, -1, -1)):
                sl = slice(c * GLA_CHUNK, (c + 1) * GLA_CHUNK)
                pieces[c] = o_intra[sl] + lax.dot_general(
                    qt[sl], st.astype(BF16), NT_DIMS, preferred_element_type=F32)
                kv_t = lax.dot_general(v[sl], ks[sl], TN_DIMS, preferred_element_type=F32)
                st = st * dec_scr[d, t, pl.ds(c, 1), :] + kv_t
            o_scrs[d][rows, :] = jnp.concatenate(pieces, axis=0)
            st_scr[d] = st
        return carry

    lax.fori_loop(0, n_tiles, scan, 0)

    def finish(r, carry):
        r0 = pl.multiple_of(r * GLA_TILE, GLA_TILE)
        rows = pl.ds(r0, GLA_TILE)
        y = _rms(of_scr[rows, :] + ob_scr[rows, :], nw_ref[...], NORM_EPS)
        gr = gr_ref[rows, :].astype(F32)
        ga = ga_ref[rows, :].astype(F32)
        out_ref[rows, :] = (_sigmoid(ga) * (y * (gr * _sigmoid(gr)))).astype(BF16)
        return carry

    lax.fori_loop(0, n_tiles, finish, 0, unroll=2)


def _gla(proj, g, nw, tri, batch, seq):
    t = proj.shape[0]
    kb = GLA_K // GLA_DK
    vb = PROJ_TILE // GLA_DV
    return pl.pallas_call(
        _gla_kernel,
        grid=(batch, GLA_HEADS),
        in_specs=[
            pl.BlockSpec((seq, GLA_DK), lambda b, h: (b, h)),
            pl.BlockSpec((seq, GLA_DK), lambda b, h: (b, kb + h)),
            pl.BlockSpec((seq, GLA_DV), lambda b, h: (b, _T_GV * vb + h)),
            pl.BlockSpec((seq, GLA_DV), lambda b, h: (b, _T_GR * vb + h)),
            pl.BlockSpec((seq, GLA_DV), lambda b, h: (b, _T_GA * vb + h)),
            pl.BlockSpec((seq, GLA_DK), lambda b, h: (b, h)),
            pl.BlockSpec((seq, GLA_DK), lambda b, h: (b, kb + h)),
            pl.BlockSpec((1, GLA_DV), lambda b, h: (0, 0)),
            pl.BlockSpec((2, GLA_TILE, GLA_TILE), lambda b, h: (0, 0, 0)),
        ],
        out_specs=pl.BlockSpec((seq, GLA_DV), lambda b, h: (b, h)),
        out_shape=jax.ShapeDtypeStruct((t, GLA_HEADS * GLA_DV), BF16),
        scratch_shapes=[
            pltpu.VMEM((2, seq, GLA_DK), BF16),
            pltpu.VMEM((2, seq, GLA_DK), BF16),
            pltpu.VMEM((2, seq, GLA_DK), BF16),
            pltpu.VMEM((2, seq // GLA_TILE, GLA_TILE // GLA_CHUNK, GLA_DK), F32),
            pltpu.VMEM((seq, GLA_DV), F32),
            pltpu.VMEM((seq, GLA_DV), F32),
            pltpu.VMEM((2, GLA_DV, GLA_DK), F32),
        ],
        compiler_params=pltpu.CompilerParams(
            dimension_semantics=("parallel", "parallel"), vmem_limit_bytes=VMEM_LIMIT),
        name="gla",
    )(proj, proj, proj, proj, proj, g, g, nw, tri)


def _diff_kernel(q_ref, k_ref, v_ref, gb_ref, za_ref, lam_ref, nw_ref, out_ref, kt_scr):
    @pl.when(pl.program_id(2) == 0)
    def _():
        kt_scr[...] = k_ref[...].T

    lv = lam_ref[...]
    lam = (jnp.exp(jnp.sum(lv[0:1] * lv[1:2], keepdims=True))
           - jnp.exp(jnp.sum(lv[2:3] * lv[3:4], keepdims=True)) + LAMBDA_INIT)
    v = v_ref[...]
    n_sub = DIFF_TQ // DIFF_ROWS
    scores, probs, sums = {}, {}, {}

    def rows_of(r):
        return slice(r * DIFF_ROWS, (r + 1) * DIFF_ROWS)

    def score(r, c):
        qc = q_ref[rows_of(r), c * DIFF_HD:(c + 1) * DIFF_HD]
        scores[r, c] = jnp.dot(qc, kt_scr[c * DIFF_HD:(c + 1) * DIFF_HD, :],
                               preferred_element_type=F32)

    def softmax(r, c):
        s = scores.pop((r, c))
        e = jnp.exp2(s - jnp.max(s, axis=-1, keepdims=True))
        sums[r, c] = jnp.sum(e, axis=-1, keepdims=True)
        probs[r, c] = e.astype(BF16)

    def values(r):
        w = (probs.pop((r, 0)) * (1.0 / sums[r, 0]).astype(BF16)
             - probs.pop((r, 1)) * (lam / sums[r, 1]).astype(BF16))
        o = jnp.concatenate(
            [jnp.dot(w[:DIFF_SUB], v, preferred_element_type=F32),
             jnp.dot(w[DIFF_SUB:], v, preferred_element_type=F32)], axis=0)
        y = _rms(o, nw_ref[...], SUBLN_EPS) * (1.0 - LAMBDA_INIT)
        gb = gb_ref[rows_of(r), :].astype(F32)
        out_ref[rows_of(r), :] = (za_ref[rows_of(r), :].astype(F32) + _sigmoid(gb) * y).astype(BF16)

    score(0, 0)
    score(0, 1)
    for r in range(n_sub):
        last = r + 1 == n_sub
        softmax(r, 0)
        if not last:
            score(r + 1, 0)
        elif r > 0:
            values(r - 1)
        softmax(r, 1)
        if not last:
            score(r + 1, 1)
            if r > 0:
                values(r - 1)
    values(n_sub - 1)


def _diff(proj, za, lam4, nw, batch, seq):
    t = proj.shape[0]
    hw = 2 * DIFF_HD
    vb = PROJ_TILE // hw
    nq = seq // DIFF_TQ
    return pl.pallas_call(
        _diff_kernel,
        grid=(batch, DIFF_HEADS, nq),
        in_specs=[
            pl.BlockSpec((DIFF_TQ, hw), lambda b, h, i: (b * nq + i, _T_DQ * vb + h)),
            pl.BlockSpec((seq, hw), lambda b, h, i: (b, _T_DK * vb + h)),
            pl.BlockSpec((seq, hw), lambda b, h, i: (b, _T_DV * vb + h)),
            pl.BlockSpec((DIFF_TQ, hw), lambda b, h, i: (b * nq + i, _T_GB * vb + h)),
            pl.BlockSpec((DIFF_TQ, hw), lambda b, h, i: (b * nq + i, h)),
            pl.BlockSpec((4, DIFF_HD), lambda b, h, i: (0, 0)),
            pl.BlockSpec((1, hw), lambda b, h, i: (0, 0)),
        ],
        out_specs=pl.BlockSpec((DIFF_TQ, hw), lambda b, h, i: (b * nq + i, h)),
        out_shape=jax.ShapeDtypeStruct((t, DIFF_HEADS * hw), BF16),
        scratch_shapes=[pltpu.VMEM((hw, seq), BF16)],
        compiler_params=pltpu.CompilerParams(
            dimension_semantics=("parallel", "parallel", "arbitrary"), vmem_limit_bytes=VMEM_LIMIT),
        name="diffattn",
    )(proj, proj, proj, proj, za, lam4, nw)


def _tail_kernel(m_ref, x_ref, wo_ref, nfw_ref, wg_ref, wu_ref, wd_ref, nlw_ref, out_ref):
    h1 = x_ref[...] + jnp.dot(m_ref[...], wo_ref[...], preferred_element_type=F32)
    n = _rms(h1, nfw_ref[...], NORM_EPS).astype(BF16)
    acc = h1
    for c in range(wg_ref.shape[0]):
        g = jnp.dot(n, wg_ref[c], preferred_element_type=F32)
        u = jnp.dot(n, wu_ref[c], preferred_element_type=F32)
        hd = ((g * _sigmoid(g)) * u).astype(BF16)
        acc = acc + jnp.dot(hd, wd_ref[c], preferred_element_type=F32)
    out_ref[...] = _rms(acc, nlw_ref[...], NORM_EPS)


def _tail(merged, x2, wo, nfw, wg, wu, wd, nlw):
    t = x2.shape[0]
    nc = wg.shape[0]
    resident = pl.Buffered(1)
    return pl.pallas_call(
        _tail_kernel,
        grid=(t // TAIL_TM,),
        in_specs=[
            pl.BlockSpec((TAIL_TM, D_MODEL), lambda i: (i, 0)),
            pl.BlockSpec((TAIL_TM, D_MODEL), lambda i: (i, 0)),
            pl.BlockSpec((D_MODEL, D_MODEL), lambda i: (0, 0), pipeline_mode=resident),
            pl.BlockSpec((1, D_MODEL), lambda i: (0, 0)),
            pl.BlockSpec((nc, D_MODEL, FFN_CHUNK), lambda i: (0, 0, 0), pipeline_mode=resident),
            pl.BlockSpec((nc, D_MODEL, FFN_CHUNK), lambda i: (0, 0, 0), pipeline_mode=resident),
            pl.BlockSpec((nc, FFN_CHUNK, D_MODEL), lambda i: (0, 0, 0), pipeline_mode=resident),
            pl.BlockSpec((1, D_MODEL), lambda i: (0, 0)),
        ],
        out_specs=pl.BlockSpec((TAIL_TM, D_MODEL), lambda i: (i, 0)),
        out_shape=jax.ShapeDtypeStruct((t, D_MODEL), F32),
        compiler_params=pltpu.CompilerParams(
            dimension_semantics=("parallel",), vmem_limit_bytes=VMEM_LIMIT),
        name="tail",
    )(merged, x2, wo, nfw, wg, wu, wd, nlw)


def _rope_tables(seq):
    inv_freq = 1.0 / (ROPE_THETA ** (jnp.arange(0, DIFF_HD, 2, dtype=F32) / DIFF_HD))
    freqs = jnp.arange(seq, dtype=F32)[:, None] * inv_freq[None, :]
    emb = jnp.concatenate([freqs, freqs], axis=-1)
    half_sign = jnp.concatenate([-jnp.ones((DIFF_HD // 2,), F32), jnp.ones((DIFF_HD // 2,), F32)])
    return jnp.cos(emb), jnp.sin(emb) * half_sign[None, :]


def _chunk_masks():
    r = jnp.arange(GLA_TILE)
    same = (r[:, None] // GLA_CHUNK) == (r[None, :] // GLA_CHUNK)
    lower = same & (r[None, :] <= r[:, None])
    upper = same & (r[None, :] >= r[:, None])
    return jnp.stack([lower, upper]).astype(BF16)


def kernel(x, norm_mix_w, w_in, w_gk2, b_gk, gla_norm_w, lambda_q1, lambda_k1, lambda_q2,
           lambda_k2, diff_subln_w, w_out, norm_ffn_w, w_ffn_in, w_ffn_out, norm_final_w):
    batch, seq, d = x.shape
    assert d == D_MODEL and seq % PROJ_TILE == 0 and w_in.shape[0] == 1
    x2 = x.reshape(batch * seq, d)

    w = w_in[0]
    o_glr = 2 * GLA_K + 2 * D_MODEL
    w_main = jnp.concatenate([w[:, :o_glr], w[:, o_glr + 2 * GLA_RANK:]], axis=1).astype(BF16)
    w_glr = jnp.pad(w[:, o_glr:o_glr + 2 * GLA_RANK], ((0, 0), (0, LANES - 2 * GLA_RANK))).astype(BF16)
    wgk = jnp.zeros((LANES, 2 * GLA_K), F32)
    wgk = (wgk.at[:GLA_RANK, :GLA_K].set(w_gk2[0, 0])
           .at[GLA_RANK:2 * GLA_RANK, GLA_K:].set(w_gk2[0, 1]).astype(BF16))
    bgk = b_gk[0].reshape(1, 2 * GLA_K)
    cos, sin = _rope_tables(seq)
    tri = _chunk_masks()
    lam4 = jnp.stack([lambda_q1[0], lambda_k1[0], lambda_q2[0], lambda_k2[0]])
    nc = FFN_HIDDEN // FFN_CHUNK
    wf = w_ffn_in[0].astype(BF16)
    wg = wf[:, :FFN_HIDDEN].reshape(d, nc, FFN_CHUNK).transpose(1, 0, 2)
    wu = wf[:, FFN_HIDDEN:].reshape(d, nc, FFN_CHUNK).transpose(1, 0, 2)
    wd = w_ffn_out[0].astype(BF16).reshape(nc, FFN_CHUNK, d)

    proj, g = _inproj(x2, norm_mix_w, w_main, w_glr, wgk, bgk, cos, sin, seq)
    za = _gla(proj, g, gla_norm_w, tri, batch, seq)
    merged = _diff(proj, za, lam4, diff_subln_w, batch, seq)
    out = _tail(merged, x2, w_out[0].astype(BF16), norm_ffn_w, wg, wu, wd,
                norm_final_w.reshape(1, d))
    return out.reshape(batch, seq, d)
```

```python
import functools
import math

import jax
import jax.numpy as jnp
import numpy as np
from jax import lax
from jax.experimental import pallas as pl
from jax.experimental.pallas import tpu as pltpu

F32 = jnp.float32
BF16 = jnp.bfloat16

D_MODEL = 1024
GLA_HEADS = 4
GLA_K = 512
GLA_DK = 128
GLA_DV = 256
GLA_RANK = 16
GLA_GATE_NORM = 16.0
GLA_CHUNK = 64
DIFF_HEADS = 4
DIFF_HD = 128
ROPE_THETA = 10000.0
FFN_HIDDEN = 2816
NORM_EPS = 1e-6
SUBLN_EPS = 1e-5
LAMBDA_INIT = 0.8 - 0.6 * math.exp(-0.3 * 0)

LANES = 128
PROJ_TILE = 1024
GLA_TILE = 256
DIFF_TQ = 1024
DIFF_ROWS = 512
DIFF_SUB = DIFF_ROWS // 2
TAIL_TM = 512
FFN_CHUNK = 256
VMEM_LIMIT = 56 * 1024 * 1024

_N_PROJ_TILES = 8
_T_QK, _T_GV, _T_GR, _T_DQ, _T_DK, _T_DV, _T_GA, _T_GB = range(_N_PROJ_TILES)

NT_DIMS = (((1,), (1,)), ((), ()))
TN_DIMS = (((0,), (0,)), ((), ()))


def _rms(x, w, eps):
    ms = jnp.mean(x * x, axis=-1, keepdims=True)
    return x * lax.rsqrt(ms + eps) * w


def _sigmoid(x):
    return 0.5 * jnp.tanh(0.5 * x) + 0.5


def _inproj_kernel(x_ref, nw_ref, w_ref, wglr_ref, wgk_ref, bgk_ref, cos_ref, sin_ref,
                   out_ref, g_ref, u_scr, glr_scr):
    j = pl.program_id(1)
    assert _T_QK == 0
    gate_rows = PROJ_TILE // _N_PROJ_TILES

    def gate(glr, rows):
        logits = jnp.dot(glr, wgk_ref[...], preferred_element_type=F32) + bgk_ref[...]
        soft = jnp.log2(1.0 + jnp.exp2(jnp.abs(logits) * (-math.log2(math.e))))
        g_ref[rows, :] = (jnp.minimum(logits, 0.0) * (1.0 / GLA_GATE_NORM)
                          - soft * (math.log(2.0) / GLA_GATE_NORM))

    def gate_step():
        rows = pl.ds(pl.multiple_of(j * gate_rows, gate_rows), gate_rows)
        gate(glr_scr[rows, :], rows)

    def project():
        gate_step()
        return jnp.dot(u_scr[...], w_ref[...], preferred_element_type=F32)

    def rope_store(scale):
        acc = project()
        cos = cos_ref[...]
        sin = sin_ref[...]
        for g in range(PROJ_TILE // LANES):
            xg = acc[:, g * LANES:(g + 1) * LANES]
            y = xg * cos + pltpu.roll(xg, LANES // 2, axis=1) * sin
            if scale != 1.0:
                y = y * scale
            out_ref[:, g * LANES:(g + 1) * LANES] = y.astype(BF16)

    @pl.when(j == _T_QK)
    def _():
        u = _rms(x_ref[...], nw_ref[...], NORM_EPS).astype(BF16)
        u_scr[...] = u
        half = PROJ_TILE // 2
        wglr = wglr_ref[...]
        glr = jnp.concatenate(
            [jnp.dot(u[:half], wglr, preferred_element_type=F32),
             jnp.dot(u[half:], wglr, preferred_element_type=F32)], axis=0).astype(BF16)
        glr_scr[...] = glr
        gate(glr[:gate_rows], slice(0, gate_rows))
        acc = jnp.dot(u, w_ref[...], preferred_element_type=F32)
        out_ref[:, :GLA_K] = (acc[:, :GLA_K] * (GLA_DK ** -0.5)).astype(BF16)
        out_ref[:, GLA_K:] = acc[:, GLA_K:].astype(BF16)

    @pl.when(j == _T_DQ)
    def _():
        rope_store((DIFF_HD ** -0.5) * math.log2(math.e))

    @pl.when(j == _T_DK)
    def _():
        rope_store(1.0)

    @pl.when((j != _T_QK) & (j != _T_DQ) & (j != _T_DK))
    def _():
        out_ref[...] = project().astype(BF16)


def _inproj(x2, nw, w_main, w_glr, wgk, bgk, cos, sin, seq):
    t = x2.shape[0]
    n_col = w_main.shape[1] // PROJ_TILE
    pos_blocks = seq // PROJ_TILE
    return pl.pallas_call(
        _inproj_kernel,
        grid=(t // PROJ_TILE, n_col),
        in_specs=[
            pl.BlockSpec((PROJ_TILE, D_MODEL), lambda i, j: (i, 0)),
            pl.BlockSpec((1, D_MODEL), lambda i, j: (0, 0)),
            pl.BlockSpec((D_MODEL, PROJ_TILE), lambda i, j: (0, j)),
            pl.BlockSpec((D_MODEL, LANES), lambda i, j: (0, 0)),
            pl.BlockSpec((LANES, 2 * GLA_K), lambda i, j: (0, 0)),
            pl.BlockSpec((1, 2 * GLA_K), lambda i, j: (0, 0)),
            pl.BlockSpec((PROJ_TILE, LANES), lambda i, j: (i % pos_blocks, 0)),
            pl.BlockSpec((PROJ_TILE, LANES), lambda i, j: (i % pos_blocks, 0)),
        ],
        out_specs=[
            pl.BlockSpec((PROJ_TILE, PROJ_TILE), lambda i, j: (i, j)),
            pl.BlockSpec((PROJ_TILE, 2 * GLA_K), lambda i, j: (i, 0)),
        ],
        out_shape=[
            jax.ShapeDtypeStruct((t, w_main.shape[1]), BF16),
            jax.ShapeDtypeStruct((t, 2 * GLA_K), F32),
        ],
        scratch_shapes=[pltpu.VMEM((PROJ_TILE, D_MODEL), BF16), pltpu.VMEM((PROJ_TILE, LANES), BF16)],
        compiler_params=pltpu.CompilerParams(
            dimension_semantics=("parallel", "arbitrary"), vmem_limit_bytes=VMEM_LIMIT),
        name="inproj",
    )(x2, nw, w_main, w_glr, wgk, bgk, cos, sin)


def _gla_kernel(q_ref, k_ref, v_ref, gr_ref, ga_ref, gf_ref, gb_ref, nw_ref, tri_ref, out_ref,
                qt_scr, kt_scr, ks_scr, dec_scr, of_scr, ob_scr, st_scr):
    seq = q_ref.shape[0]
    n_tiles = seq // GLA_TILE
    per_tile = GLA_TILE // GLA_CHUNK
    g_refs = (gf_ref, gb_ref)
    o_scrs = (of_scr, ob_scr)

    def decays(r, carry):
        r0 = pl.multiple_of(r * GLA_TILE, GLA_TILE)
        rows = pl.ds(r0, GLA_TILE)
        q = q_ref[rows, :].astype(F32)
        k = k_ref[rows, :].astype(F32)
        for d in range(2):
            g = g_refs[d][rows, :]
            hi = g.astype(BF16)
            lo = (g - hi.astype(F32)).astype(BF16)
            c2 = jnp.dot(tri_ref[d], jnp.concatenate([hi, lo], axis=1), preferred_element_type=F32)
            bcum = c2[:, :LANES] + c2[:, LANES:]
            edge = GLA_CHUNK - 1 if d == 0 else 0
            tots = [bcum[c * GLA_CHUNK + edge:c * GLA_CHUNK + edge + 1, :] for c in range(per_tile)]
            tot = jnp.concatenate([jnp.broadcast_to(tc, (GLA_CHUNK, GLA_DK)) for tc in tots], axis=0)
            qt_scr[d, rows, :] = (q * jnp.exp(bcum)).astype(BF16)
            kt_scr[d, rows, :] = (k * jnp.exp(-bcum)).astype(BF16)
            ks_scr[d, rows, :] = (k * jnp.exp(tot - bcum)).astype(BF16)
            for c in range(per_tile):
                dec_scr[d, r, pl.ds(c, 1), :] = jnp.exp(tots[c])
        return carry

    lax.fori_loop(0, n_tiles, decays, 0, unroll=4)

    st_scr[...] = jnp.zeros(st_scr.shape, F32)

    def scan(n, carry):
        for d in range(2):
            t = n if d == 0 else n_tiles - 1 - n
            base = pl.multiple_of(t * GLA_TILE, GLA_TILE)
            rows = pl.ds(base, GLA_TILE)
            qt = qt_scr[d, rows, :]
            ks = ks_scr[d, rows, :]
            v = v_ref[rows, :]
            a = lax.dot_general(qt, kt_scr[d, rows, :], NT_DIMS, preferred_element_type=F32)
            a = jnp.where(tri_ref[d] > 0, a, 0.0).astype(BF16)
            o_intra = jnp.dot(a, v, preferred_element_type=F32)
            st = st_scr[d]
            pieces = [None] * per_tile
            for c in (range(per_tile) if d == 0 else range(per_tile - 1, -1, -1)):
                sl = slice(c * GLA_CHUNK, (c + 1) * GLA_CHUNK)
                pieces[c] = o_intra[sl] + lax.dot_general(
                    qt[sl], st.astype(BF16), NT_DIMS, preferred_element_type=F32)
                kv_t = lax.dot_general(v[sl], ks[sl], TN_DIMS, preferred_element_type=F32)
                st = st * dec_scr[d, t, pl.ds(c, 1), :] + kv_t
            o_scrs[d][rows, :] = jnp.concatenate(pieces, axis=0)
            st_scr[d] = st
        return carry

    lax.fori_loop(0, n_tiles, scan, 0, unroll=8)

    def finish(r, carry):
        r0 = pl.multiple_of(r * GLA_TILE, GLA_TILE)
        rows = pl.ds(r0, GLA_TILE)
        y = _rms(of_scr[rows, :] + ob_scr[rows, :], nw_ref[...], NORM_EPS)
        gr = gr_ref[rows, :].astype(F32)
        ga = ga_ref[rows, :].astype(F32)
        out_ref[rows, :] = (_sigmoid(ga) * (y * (gr * _sigmoid(gr)))).astype(BF16)
        return carry

    lax.fori_loop(0, n_tiles, finish, 0, unroll=2)


def _gla(proj, g, nw, tri, batch, seq):
    t = proj.shape[0]
    kb = GLA_K // GLA_DK
    vb = PROJ_TILE // GLA_DV
    return pl.pallas_call(
        _gla_kernel,
        grid=(batch, GLA_HEADS),
        in_specs=[
            pl.BlockSpec((seq, GLA_DK), lambda b, h: (b, h)),
            pl.BlockSpec((seq, GLA_DK), lambda b, h: (b, kb + h)),
            pl.BlockSpec((seq, GLA_DV), lambda b, h: (b, _T_GV * vb + h)),
            pl.BlockSpec((seq, GLA_DV), lambda b, h: (b, _T_GR * vb + h)),
            pl.BlockSpec((seq, GLA_DV), lambda b, h: (b, _T_GA * vb + h)),
            pl.BlockSpec((seq, GLA_DK), lambda b, h: (b, h)),
            pl.BlockSpec((seq, GLA_DK), lambda b, h: (b, kb + h)),
            pl.BlockSpec((1, GLA_DV), lambda b, h: (0, 0)),
            pl.BlockSpec((2, GLA_TILE, GLA_TILE), lambda b, h: (0, 0, 0)),
        ],
        out_specs=pl.BlockSpec((seq, GLA_DV), lambda b, h: (b, h)),
        out_shape=jax.ShapeDtypeStruct((t, GLA_HEADS * GLA_DV), BF16),
        scratch_shapes=[
            pltpu.VMEM((2, seq, GLA_DK), BF16),
            pltpu.VMEM((2, seq, GLA_DK), BF16),
            pltpu.VMEM((2, seq, GLA_DK), BF16),
            pltpu.VMEM((2, seq // GLA_TILE, GLA_TILE // GLA_CHUNK, GLA_DK), F32),
            pltpu.VMEM((seq, GLA_DV), F32),
            pltpu.VMEM((seq, GLA_DV), F32),
            pltpu.VMEM((2, GLA_DV, GLA_DK), F32),
        ],
        compiler_params=pltpu.CompilerParams(
            dimension_semantics=("parallel", "parallel"), vmem_limit_bytes=VMEM_LIMIT),
        name="gla",
    )(proj, proj, proj, proj, proj, g, g, nw, tri)


def _diff_kernel(q_ref, k_ref, v_ref, gb_ref, za_ref, lam_ref, nw_ref, out_ref, kt_scr):
    @pl.when(pl.program_id(2) == 0)
    def _():
        kt_scr[...] = k_ref[...].T

    lv = lam_ref[...]
    lam = (jnp.exp(jnp.sum(lv[0:1] * lv[1:2], keepdims=True))
           - jnp.exp(jnp.sum(lv[2:3] * lv[3:4], keepdims=True)) + LAMBDA_INIT)
    v = v_ref[...]
    n_sub = DIFF_TQ // DIFF_ROWS
    scores, probs, sums = {}, {}, {}

    def rows_of(r):
        return slice(r * DIFF_ROWS, (r + 1) * DIFF_ROWS)

    def score(r, c):
        qc = q_ref[rows_of(r), c * DIFF_HD:(c + 1) * DIFF_HD]
        scores[r, c] = jnp.dot(qc, kt_scr[c * DIFF_HD:(c + 1) * DIFF_HD, :],
                               preferred_element_type=F32)

    def softmax(r, c):
        s = scores.pop((r, c))
        e = jnp.exp2(s - jnp.max(s, axis=-1, keepdims=True))
        sums[r, c] = jnp.sum(e, axis=-1, keepdims=True)
        probs[r, c] = e.astype(BF16)

    def values(r):
        w = (probs.pop((r, 0)) * (1.0 / sums[r, 0]).astype(BF16)
             - probs.pop((r, 1)) * (lam / sums[r, 1]).astype(BF16))
        o = jnp.concatenate(
            [jnp.dot(w[:DIFF_SUB], v, preferred_element_type=F32),
             jnp.dot(w[DIFF_SUB:], v, preferred_element_type=F32)], axis=0)
        y = _rms(o, nw_ref[...], SUBLN_EPS) * (1.0 - LAMBDA_INIT)
        gb = gb_ref[rows_of(r), :].astype(F32)
        out_ref[rows_of(r), :] = (za_ref[rows_of(r), :].astype(F32) + _sigmoid(gb) * y).astype(BF16)

    score(0, 0)
    score(0, 1)
    for r in range(n_sub):
        last = r + 1 == n_sub
        softmax(r, 0)
        if not last:
            score(r + 1, 0)
        elif r > 0:
            values(r - 1)
        softmax(r, 1)
        if not last:
            score(r + 1, 1)
            if r > 0:
                values(r - 1)
    values(n_sub - 1)


def _diff(proj, za, lam4, nw, batch, seq):
    t = proj.shape[0]
    hw = 2 * DIFF_HD
    vb = PROJ_TILE // hw
    nq = seq // DIFF_TQ
    return pl.pallas_call(
        _diff_kernel,
        grid=(batch, DIFF_HEADS, nq),
        in_specs=[
            pl.BlockSpec((DIFF_TQ, hw), lambda b, h, i: (b * nq + i, _T_DQ * vb + h)),
            pl.BlockSpec((seq, hw), lambda b, h, i: (b, _T_DK * vb + h)),
            pl.BlockSpec((seq, hw), lambda b, h, i: (b, _T_DV * vb + h)),
            pl.BlockSpec((DIFF_TQ, hw), lambda b, h, i: (b * nq + i, _T_GB * vb + h)),
            pl.BlockSpec((DIFF_TQ, hw), lambda b, h, i: (b * nq + i, h)),
            pl.BlockSpec((4, DIFF_HD), lambda b, h, i: (0, 0)),
            pl.BlockSpec((1, hw), lambda b, h, i: (0, 0)),
        ],
        out_specs=pl.BlockSpec((DIFF_TQ, hw), lambda b, h, i: (b * nq + i, h)),
        out_shape=jax.ShapeDtypeStruct((t, DIFF_HEADS * hw), BF16),
        scratch_shapes=[pltpu.VMEM((hw, seq), BF16)],
        compiler_params=pltpu.CompilerParams(
            dimension_semantics=("parallel", "parallel", "arbitrary"), vmem_limit_bytes=VMEM_LIMIT),
        name="diffattn",
    )(proj, proj, proj, proj, za, lam4, nw)


def _tail_kernel(m_ref, x_ref, wo_ref, nfw_ref, wi_ref, wd_ref, nlw_ref, out_ref):
    h1 = x_ref[...] + jnp.dot(m_ref[...], wo_ref[...], preferred_element_type=F32)
    n = _rms(h1, nfw_ref[...], NORM_EPS).astype(BF16)
    acc = h1
    for c in range(FFN_HIDDEN // FFN_CHUNK):
        cols = slice(c * FFN_CHUNK, (c + 1) * FFN_CHUNK)
        up_cols = slice(FFN_HIDDEN + c * FFN_CHUNK, FFN_HIDDEN + (c + 1) * FFN_CHUNK)
        g = jnp.dot(n, wi_ref[:, cols], preferred_element_type=F32)
        u = jnp.dot(n, wi_ref[:, up_cols], preferred_element_type=F32)
        hd = ((g * _sigmoid(g)) * u).astype(BF16)
        acc = acc + jnp.dot(hd, wd_ref[cols, :], preferred_element_type=F32)
    out_ref[...] = _rms(acc, nlw_ref[...], NORM_EPS)


def _tail(merged, x2, wo, nfw, wi, wd, nlw):
    t = x2.shape[0]
    resident = pl.Buffered(1)
    return pl.pallas_call(
        _tail_kernel,
        grid=(t // TAIL_TM,),
        in_specs=[
            pl.BlockSpec((TAIL_TM, D_MODEL), lambda i: (i, 0)),
            pl.BlockSpec((TAIL_TM, D_MODEL), lambda i: (i, 0)),
            pl.BlockSpec((D_MODEL, D_MODEL), lambda i: (0, 0), pipeline_mode=resident),
            pl.BlockSpec((1, D_MODEL), lambda i: (0, 0)),
            pl.BlockSpec((D_MODEL, 2 * FFN_HIDDEN), lambda i: (0, 0), pipeline_mode=resident),
            pl.BlockSpec((FFN_HIDDEN, D_MODEL), lambda i: (0, 0), pipeline_mode=resident),
            pl.BlockSpec((1, D_MODEL), lambda i: (0, 0)),
        ],
        out_specs=pl.BlockSpec((TAIL_TM, D_MODEL), lambda i: (i, 0)),
        out_shape=jax.ShapeDtypeStruct((t, D_MODEL), F32),
        compiler_params=pltpu.CompilerParams(
            dimension_semantics=("parallel",), vmem_limit_bytes=VMEM_LIMIT),
        name="tail",
    )(merged, x2, wo, nfw, wi, wd, nlw)


def _rope_tables(seq):
    inv_freq = 1.0 / (ROPE_THETA ** (np.arange(0, DIFF_HD, 2, dtype=np.float32) / DIFF_HD))
    freqs = np.arange(seq, dtype=np.float32)[:, None] * inv_freq[None, :].astype(np.float32)
    emb = np.concatenate([freqs, freqs], axis=-1)
    half_sign = np.concatenate([-np.ones(DIFF_HD // 2, np.float32), np.ones(DIFF_HD // 2, np.float32)])
    return (jnp.asarray(np.cos(emb), dtype=F32),
            jnp.asarray(np.sin(emb) * half_sign[None, :], dtype=F32))


def _chunk_masks():
    r = np.arange(GLA_TILE)
    same = (r[:, None] // GLA_CHUNK) == (r[None, :] // GLA_CHUNK)
    lower = same & (r[None, :] <= r[:, None])
    upper = same & (r[None, :] >= r[:, None])
    return jnp.asarray(np.stack([lower, upper]), dtype=BF16)


def kernel(x, norm_mix_w, w_in, w_gk2, b_gk, gla_norm_w, lambda_q1, lambda_k1, lambda_q2,
           lambda_k2, diff_subln_w, w_out, norm_ffn_w, w_ffn_in, w_ffn_out, norm_final_w):
    batch, seq, d = x.shape
    assert d == D_MODEL and seq % PROJ_TILE == 0 and w_in.shape[0] == 1
    x2 = x.reshape(batch * seq, d)

    w = w_in[0]
    o_glr = 2 * GLA_K + 2 * D_MODEL
    w_main = jnp.concatenate([w[:, :o_glr], w[:, o_glr + 2 * GLA_RANK:]], axis=1).astype(BF16)
    w_glr = jnp.pad(w[:, o_glr:o_glr + 2 * GLA_RANK], ((0, 0), (0, LANES - 2 * GLA_RANK))).astype(BF16)
    wgk = jnp.zeros((LANES, 2 * GLA_K), F32)
    wgk = (wgk.at[:GLA_RANK, :GLA_K].set(w_gk2[0, 0])
           .at[GLA_RANK:2 * GLA_RANK, GLA_K:].set(w_gk2[0, 1]).astype(BF16))
    bgk = b_gk[0].reshape(1, 2 * GLA_K)
    cos, sin = _rope_tables(seq)
    tri = _chunk_masks()
    lam4 = jnp.stack([lambda_q1[0], lambda_k1[0], lambda_q2[0], lambda_k2[0]])

    proj, g = _inproj(x2, norm_mix_w, w_main, w_glr, wgk, bgk, cos, sin, seq)
    za = _gla(proj, g, gla_norm_w, tri, batch, seq)
    merged = _diff(proj, za, lam4, diff_subln_w, batch, seq)
    out = _tail(merged, x2, w_out[0].astype(BF16), norm_ffn_w, w_ffn_in[0].astype(BF16),
                w_ffn_out[0].astype(BF16), norm_final_w.reshape(1, d))
    return out.reshape(batch, seq, d)
```

```python
import functools
import math

import jax
import jax.numpy as jnp
import numpy as np
from jax import lax
from jax.experimental import pallas as pl
from jax.experimental.pallas import tpu as pltpu

F32 = jnp.float32
BF16 = jnp.bfloat16

D_MODEL = 1024
GLA_HEADS = 4
GLA_K = 512
GLA_DK = 128
GLA_DV = 256
GLA_RANK = 16
GLA_GATE_NORM = 16.0
GLA_CHUNK = 64
DIFF_HEADS = 4
DIFF_HD = 128
ROPE_THETA = 10000.0
FFN_HIDDEN = 2816
NORM_EPS = 1e-6
SUBLN_EPS = 1e-5
LAMBDA_INIT = 0.8 - 0.6 * math.exp(-0.3 * 0)

LANES = 128
PROJ_TILE = 1024
PROJ_ROWS = 512
GLA_TILE = 256
DIFF_TQ = 1024
DIFF_ROWS = 512
DIFF_SUB = DIFF_ROWS // 2
TAIL_TM = 512
FFN_CHUNK = 256
VMEM_LIMIT = 56 * 1024 * 1024

_N_PROJ_TILES = 8
_T_QK, _T_GV, _T_GR, _T_DQ, _T_DK, _T_DV, _T_GA, _T_GB = range(_N_PROJ_TILES)

NT_DIMS = (((1,), (1,)), ((), ()))
TN_DIMS = (((0,), (0,)), ((), ()))


def _rms(x, w, eps):
    ms = jnp.mean(x * x, axis=-1, keepdims=True)
    return x * lax.rsqrt(ms + eps) * w


def _sigmoid(x):
    return 0.5 * jnp.tanh(0.5 * x) + 0.5


def _inproj_kernel(x_ref, nw_ref, w_ref, wglr_ref, wgk_ref, bgk_ref, cos_ref, sin_ref,
                   out_ref, g_ref):
    u = _rms(x_ref[...], nw_ref[...], NORM_EPS).astype(BF16)
    cos = cos_ref[...]
    sin = sin_ref[...]
    rope_scale = {_T_DQ: (DIFF_HD ** -0.5) * math.log2(math.e), _T_DK: 1.0}

    def gate():
        glr = jnp.dot(u, wglr_ref[...], preferred_element_type=F32).astype(BF16)
        logits = jnp.dot(glr, wgk_ref[...], preferred_element_type=F32) + bgk_ref[...]
        soft = jnp.log2(1.0 + jnp.exp2(jnp.abs(logits) * (-math.log2(math.e))))
        g_ref[...] = (jnp.minimum(logits, 0.0) * (1.0 / GLA_GATE_NORM)
                      - soft * (math.log(2.0) / GLA_GATE_NORM))

    for j in range(_N_PROJ_TILES):
        cols = slice(j * PROJ_TILE, (j + 1) * PROJ_TILE)
        acc = jnp.dot(u, w_ref[:, cols], preferred_element_type=F32)
        if j == _T_QK:
            out_ref[:, :GLA_K] = (acc[:, :GLA_K] * (GLA_DK ** -0.5)).astype(BF16)
            out_ref[:, GLA_K:PROJ_TILE] = acc[:, GLA_K:].astype(BF16)
        elif j in rope_scale:
            for g in range(PROJ_TILE // LANES):
                xg = acc[:, g * LANES:(g + 1) * LANES]
                y = xg * cos + pltpu.roll(xg, LANES // 2, axis=1) * sin
                if rope_scale[j] != 1.0:
                    y = y * rope_scale[j]
                out_ref[:, j * PROJ_TILE + g * LANES:j * PROJ_TILE + (g + 1) * LANES] = y.astype(BF16)
        else:
            out_ref[:, cols] = acc.astype(BF16)
        if j == 1:
            gate()


def _inproj(x2, nw, w_main, w_glr, wgk, bgk, cos, sin, seq):
    t = x2.shape[0]
    width = w_main.shape[1]
    assert width == _N_PROJ_TILES * PROJ_TILE
    pos_blocks = seq // PROJ_ROWS
    resident = pl.Buffered(1)
    return pl.pallas_call(
        _inproj_kernel,
        grid=(t // PROJ_ROWS,),
        in_specs=[
            pl.BlockSpec((PROJ_ROWS, D_MODEL), lambda i: (i, 0)),
            pl.BlockSpec((1, D_MODEL), lambda i: (0, 0)),
            pl.BlockSpec((D_MODEL, width), lambda i: (0, 0), pipeline_mode=resident),
            pl.BlockSpec((D_MODEL, LANES), lambda i: (0, 0), pipeline_mode=resident),
            pl.BlockSpec((LANES, 2 * GLA_K), lambda i: (0, 0), pipeline_mode=resident),
            pl.BlockSpec((1, 2 * GLA_K), lambda i: (0, 0)),
            pl.BlockSpec((PROJ_ROWS, LANES), lambda i: (i % pos_blocks, 0)),
            pl.BlockSpec((PROJ_ROWS, LANES), lambda i: (i % pos_blocks, 0)),
        ],
        out_specs=[
            pl.BlockSpec((PROJ_ROWS, width), lambda i: (i, 0)),
            pl.BlockSpec((PROJ_ROWS, 2 * GLA_K), lambda i: (i, 0)),
        ],
        out_shape=[
            jax.ShapeDtypeStruct((t, width), BF16),
            jax.ShapeDtypeStruct((t, 2 * GLA_K), F32),
        ],
        compiler_params=pltpu.CompilerParams(
            dimension_semantics=("parallel",), vmem_limit_bytes=VMEM_LIMIT),
        name="inproj",
    )(x2, nw, w_main, w_glr, wgk, bgk, cos, sin)


def _gla_kernel(q_ref, k_ref, v_ref, gr_ref, ga_ref, gf_ref, gb_ref, nw_ref, tri_ref, out_ref,
                qt_scr, kt_scr, ks_scr, dec_scr, of_scr, ob_scr, st_scr):
    seq = q_ref.shape[0]
    n_tiles = seq // GLA_TILE
    per_tile = GLA_TILE // GLA_CHUNK
    g_refs = (gf_ref, gb_ref)
    o_scrs = (of_scr, ob_scr)

    def decays(r, carry):
        r0 = pl.multiple_of(r * GLA_TILE, GLA_TILE)
        rows = pl.ds(r0, GLA_TILE)
        q = q_ref[rows, :].astype(F32)
        k = k_ref[rows, :].astype(F32)
        for d in range(2):
            g = g_refs[d][rows, :]
            hi = g.astype(BF16)
            lo = (g - hi.astype(F32)).astype(BF16)
            c2 = jnp.dot(tri_ref[d], jnp.concatenate([hi, lo], axis=1), preferred_element_type=F32)
            bcum = c2[:, :LANES] + c2[:, LANES:]
            edge = GLA_CHUNK - 1 if d == 0 else 0
            tots = [bcum[c * GLA_CHUNK + edge:c * GLA_CHUNK + edge + 1, :] for c in range(per_tile)]
            tot = jnp.concatenate([jnp.broadcast_to(tc, (GLA_CHUNK, GLA_DK)) for tc in tots], axis=0)
            qt_scr[d, rows, :] = (q * jnp.exp(bcum)).astype(BF16)
            kt_scr[d, rows, :] = (k * jnp.exp(-bcum)).astype(BF16)
            ks_scr[d, rows, :] = (k * jnp.exp(tot - bcum)).astype(BF16)
            for c in range(per_tile):
                dec_scr[d, r, pl.ds(c, 1), :] = jnp.exp(tots[c])
        return carry

    lax.fori_loop(0, n_tiles, decays, 0, unroll=4)

    st_scr[...] = jnp.zeros(st_scr.shape, F32)

    def scan(n, carry):
        for d in range(2):
            t = n if d == 0 else n_tiles - 1 - n
            base = pl.multiple_of(t * GLA_TILE, GLA_TILE)
            rows = pl.ds(base, GLA_TILE)
            qt = qt_scr[d, rows, :]
            ks = ks_scr[d, rows, :]
            v = v_ref[rows, :]
            a = lax.dot_general(qt, kt_scr[d, rows, :], NT_DIMS, preferred_element_type=F32)
            a = jnp.where(tri_ref[d] > 0, a, 0.0).astype(BF16)
            o_intra = jnp.dot(a, v, preferred_element_type=F32)
            st = st_scr[d]
            pieces = [None] * per_tile
            for c in (range(per_tile) if d == 0 else range(per_tile - 1, -1, -1)):
                sl = slice(c * GLA_CHUNK, (c + 1) * GLA_CHUNK)
                pieces[c] = o_intra[sl] + lax.dot_general(
                    qt[sl], st.astype(BF16), NT_DIMS, preferred_element_type=F32)
                kv_t = lax.dot_general(v[sl], ks[sl], TN_DIMS, preferred_element_type=F32)
                st = st * dec_scr[d, t, pl.ds(c, 1), :] + kv_t
            o_scrs[d][rows, :] = jnp.concatenate(pieces, axis=0)
            st_scr[d] = st
        return carry

    lax.fori_loop(0, n_tiles, scan, 0, unroll=8)

    def finish(r, carry):
        r0 = pl.multiple_of(r * GLA_TILE, GLA_TILE)
        rows = pl.ds(r0, GLA_TILE)
        y = _rms(of_scr[rows, :] + ob_scr[rows, :], nw_ref[...], NORM_EPS)
        gr = gr_ref[rows, :].astype(F32)
        ga = ga_ref[rows, :].astype(F32)
        out_ref[rows, :] = (_sigmoid(ga) * (y * (gr * _sigmoid(gr)))).astype(BF16)
        return carry

    lax.fori_loop(0, n_tiles, finish, 0, unroll=2)


def _gla(proj, g, nw, tri, batch, seq):
    t = proj.shape[0]
    kb = GLA_K // GLA_DK
    vb = PROJ_TILE // GLA_DV
    return pl.pallas_call(
        _gla_kernel,
        grid=(batch, GLA_HEADS),
        in_specs=[
            pl.BlockSpec((seq, GLA_DK), lambda b, h: (b, h)),
            pl.BlockSpec((seq, GLA_DK), lambda b, h: (b, kb + h)),
            pl.BlockSpec((seq, GLA_DV), lambda b, h: (b, _T_GV * vb + h)),
            pl.BlockSpec((seq, GLA_DV), lambda b, h: (b, _T_GR * vb + h)),
            pl.BlockSpec((seq, GLA_DV), lambda b, h: (b, _T_GA * vb + h)),
            pl.BlockSpec((seq, GLA_DK), lambda b, h: (b, h)),
            pl.BlockSpec((seq, GLA_DK), lambda b, h: (b, kb + h)),
            pl.BlockSpec((1, GLA_DV), lambda b, h: (0, 0)),
            pl.BlockSpec((2, GLA_TILE, GLA_TILE), lambda b, h: (0, 0, 0)),
        ],
        out_specs=pl.BlockSpec((seq, GLA_DV), lambda b, h: (b, h)),
        out_shape=jax.ShapeDtypeStruct((t, GLA_HEADS * GLA_DV), BF16),
        scratch_shapes=[
            pltpu.VMEM((2, seq, GLA_DK), BF16),
            pltpu.VMEM((2, seq, GLA_DK), BF16),
            pltpu.VMEM((2, seq, GLA_DK), BF16),
            pltpu.VMEM((2, seq // GLA_TILE, GLA_TILE // GLA_CHUNK, GLA_DK), F32),
            pltpu.VMEM((seq, GLA_DV), F32),
            pltpu.VMEM((seq, GLA_DV), F32),
            pltpu.VMEM((2, GLA_DV, GLA_DK), F32),
        ],
        compiler_params=pltpu.CompilerParams(
            dimension_semantics=("parallel", "parallel"), vmem_limit_bytes=VMEM_LIMIT),
        name="gla",
    )(proj, proj, proj, proj, proj, g, g, nw, tri)


def _diff_kernel(q_ref, k_ref, v_ref, gb_ref, za_ref, lam_ref, nw_ref, out_ref, kt_scr):
    @pl.when(pl.program_id(2) == 0)
    def _():
        kt_scr[...] = k_ref[...].T

    lv = lam_ref[...]
    lam = (jnp.exp(jnp.sum(lv[0:1] * lv[1:2], keepdims=True))
           - jnp.exp(jnp.sum(lv[2:3] * lv[3:4], keepdims=True)) + LAMBDA_INIT)
    v = v_ref[...]
    n_sub = DIFF_TQ // DIFF_ROWS
    scores, probs, sums = {}, {}, {}

    def rows_of(r):
        return slice(r * DIFF_ROWS, (r + 1) * DIFF_ROWS)

    def score(r, c):
        qc = q_ref[rows_of(r), c * DIFF_HD:(c + 1) * DIFF_HD]
        scores[r, c] = jnp.dot(qc, kt_scr[c * DIFF_HD:(c + 1) * DIFF_HD, :],
                               preferred_element_type=F32)

    def softmax(r, c):
        s = scores.pop((r, c))
        e = jnp.exp2(s - jnp.max(s, axis=-1, keepdims=True))
        sums[r, c] = jnp.sum(e, axis=-1, keepdims=True)
        probs[r, c] = e.astype(BF16)

    def values(r):
        w = (probs.pop((r, 0)) * (1.0 / sums[r, 0]).astype(BF16)
             - probs.pop((r, 1)) * (lam / sums[r, 1]).astype(BF16))
        o = jnp.concatenate(
            [jnp.dot(w[:DIFF_SUB], v, preferred_element_type=F32),
             jnp.dot(w[DIFF_SUB:], v, preferred_element_type=F32)], axis=0)
        y = _rms(o, nw_ref[...], SUBLN_EPS) * (1.0 - LAMBDA_INIT)
        gb = gb_ref[rows_of(r), :].astype(F32)
        out_ref[rows_of(r), :] = (za_ref[rows_of(r), :].astype(F32) + _sigmoid(gb) * y).astype(BF16)

    score(0, 0)
    score(0, 1)
    for r in range(n_sub):
        last = r + 1 == n_sub
        softmax(r, 0)
        if not last:
            score(r + 1, 0)
        elif r > 0:
            values(r - 1)
        softmax(r, 1)
        if not last:
            score(r + 1, 1)
            if r > 0:
                values(r - 1)
    values(n_sub - 1)


def _diff(proj, za, lam4, nw, batch, seq):
    t = proj.shape[0]
    hw = 2 * DIFF_HD
    vb = PROJ_TILE // hw
    nq = seq // DIFF_TQ
    return pl.pallas_call(
        _diff_kernel,
        grid=(batch, DIFF_HEADS, nq),
        in_specs=[
            pl.BlockSpec((DIFF_TQ, hw), lambda b, h, i: (b * nq + i, _T_DQ * vb + h)),
            pl.BlockSpec((seq, hw), lambda b, h, i: (b, _T_DK * vb + h)),
            pl.BlockSpec((seq, hw), lambda b, h, i: (b, _T_DV * vb + h)),
            pl.BlockSpec((DIFF_TQ, hw), lambda b, h, i: (b * nq + i, _T_GB * vb + h)),
            pl.BlockSpec((DIFF_TQ, hw), lambda b, h, i: (b * nq + i, h)),
            pl.BlockSpec((4, DIFF_HD), lambda b, h, i: (0, 0)),
            pl.BlockSpec((1, hw), lambda b, h, i: (0, 0)),
        ],
        out_specs=pl.BlockSpec((DIFF_TQ, hw), lambda b, h, i: (b * nq + i, h)),
        out_shape=jax.ShapeDtypeStruct((t, DIFF_HEADS * hw), BF16),
        scratch_shapes=[pltpu.VMEM((hw, seq), BF16)],
        compiler_params=pltpu.CompilerParams(
            dimension_semantics=("parallel", "parallel", "arbitrary"), vmem_limit_bytes=VMEM_LIMIT),
        name="diffattn",
    )(proj, proj, proj, proj, za, lam4, nw)


def _tail_kernel(m_ref, x_ref, wo_ref, nfw_ref, wi_ref, wd_ref, nlw_ref, out_ref):
    h1 = x_ref[...] + jnp.dot(m_ref[...], wo_ref[...], preferred_element_type=F32)
    n = _rms(h1, nfw_ref[...], NORM_EPS).astype(BF16)
    acc = h1
    for c in range(FFN_HIDDEN // FFN_CHUNK):
        cols = slice(c * FFN_CHUNK, (c + 1) * FFN_CHUNK)
        up_cols = slice(FFN_HIDDEN + c * FFN_CHUNK, FFN_HIDDEN + (c + 1) * FFN_CHUNK)
        g = jnp.dot(n, wi_ref[:, cols], preferred_element_type=F32)
        u = jnp.dot(n, wi_ref[:, up_cols], preferred_element_type=F32)
        hd = ((g * _sigmoid(g)) * u).astype(BF16)
        acc = acc + jnp.dot(hd, wd_ref[cols, :], preferred_element_type=F32)
    out_ref[...] = _rms(acc, nlw_ref[...], NORM_EPS)


def _tail(merged, x2, wo, nfw, wi, wd, nlw):
    t = x2.shape[0]
    resident = pl.Buffered(1)
    return pl.pallas_call(
        _tail_kernel,
        grid=(t // TAIL_TM,),
        in_specs=[
            pl.BlockSpec((TAIL_TM, D_MODEL), lambda i: (i, 0)),
            pl.BlockSpec((TAIL_TM, D_MODEL), lambda i: (i, 0)),
            pl.BlockSpec((D_MODEL, D_MODEL), lambda i: (0, 0), pipeline_mode=resident),
            pl.BlockSpec((1, D_MODEL), lambda i: (0, 0)),
            pl.BlockSpec((D_MODEL, 2 * FFN_HIDDEN), lambda i: (0, 0), pipeline_mode=resident),
            pl.BlockSpec((FFN_HIDDEN, D_MODEL), lambda i: (0, 0), pipeline_mode=resident),
            pl.BlockSpec((1, D_MODEL), lambda i: (0, 0)),
        ],
        out_specs=pl.BlockSpec((TAIL_TM, D_MODEL), lambda i: (i, 0)),
        out_shape=jax.ShapeDtypeStruct((t, D_MODEL), F32),
        compiler_params=pltpu.CompilerParams(
            dimension_semantics=("parallel",), vmem_limit_bytes=VMEM_LIMIT),
        name="tail",
    )(merged, x2, wo, nfw, wi, wd, nlw)


def _rope_tables(seq):
    inv_freq = 1.0 / (ROPE_THETA ** (np.arange(0, DIFF_HD, 2, dtype=np.float32) / DIFF_HD))
    freqs = np.arange(seq, dtype=np.float32)[:, None] * inv_freq[None, :].astype(np.float32)
    emb = np.concatenate([freqs, freqs], axis=-1)
    half_sign = np.concatenate([-np.ones(DIFF_HD // 2, np.float32), np.ones(DIFF_HD // 2, np.float32)])
    return (jnp.asarray(np.cos(emb), dtype=F32),
            jnp.asarray(np.sin(emb) * half_sign[None, :], dtype=F32))


def _chunk_masks():
    r = np.arange(GLA_TILE)
    same = (r[:, None] // GLA_CHUNK) == (r[None, :] // GLA_CHUNK)
    lower = same & (r[None, :] <= r[:, None])
    upper = same & (r[None, :] >= r[:, None])
    return jnp.asarray(np.stack([lower, upper]), dtype=BF16)


def kernel(x, norm_mix_w, w_in, w_gk2, b_gk, gla_norm_w, lambda_q1, lambda_k1, lambda_q2,
           lambda_k2, diff_subln_w, w_out, norm_ffn_w, w_ffn_in, w_ffn_out, norm_final_w):
    batch, seq, d = x.shape
    assert d == D_MODEL and seq % PROJ_TILE == 0 and w_in.shape[0] == 1
    x2 = x.reshape(batch * seq, d)

    w = w_in[0]
    o_glr = 2 * GLA_K + 2 * D_MODEL
    w_main = jnp.concatenate([w[:, :o_glr], w[:, o_glr + 2 * GLA_RANK:]], axis=1).astype(BF16)
    w_glr = jnp.pad(w[:, o_glr:o_glr + 2 * GLA_RANK], ((0, 0), (0, LANES - 2 * GLA_RANK))).astype(BF16)
    wgk = jnp.zeros((LANES, 2 * GLA_K), F32)
    wgk = (wgk.at[:GLA_RANK, :GLA_K].set(w_gk2[0, 0])
           .at[GLA_RANK:2 * GLA_RANK, GLA_K:].set(w_gk2[0, 1]).astype(BF16))
    bgk = b_gk[0].reshape(1, 2 * GLA_K)
    cos, sin = _rope_tables(seq)
    tri = _chunk_masks()
    lam4 = jnp.stack([lambda_q1[0], lambda_k1[0], lambda_q2[0], lambda_k2[0]])

    proj, g = _inproj(x2, norm_mix_w, w_main, w_glr, wgk, bgk, cos, sin, seq)
    za = _gla(proj, g, gla_norm_w, tri, batch, seq)
    merged = _diff(proj, za, lam4, diff_subln_w, batch, seq)
    out = _tail(merged, x2, w_out[0].astype(BF16), norm_ffn_w, w_ffn_in[0].astype(BF16),
                w_ffn_out[0].astype(BF16), norm_final_w.reshape(1, d))
    return out.reshape(batch, seq, d)
```

```python
import functools
import math

import jax
import jax.numpy as jnp
import numpy as np
from jax import lax
from jax.experimental import pallas as pl
from jax.experimental.pallas import tpu as pltpu

F32 = jnp.float32
BF16 = jnp.bfloat16

D_MODEL = 1024
GLA_HEADS = 4
GLA_K = 512
GLA_DK = 128
GLA_DV = 256
GLA_RANK = 16
GLA_GATE_NORM = 16.0
GLA_CHUNK = 64
DIFF_HEADS = 4
DIFF_HD = 128
ROPE_THETA = 10000.0
FFN_HIDDEN = 2816
NORM_EPS = 1e-6
SUBLN_EPS = 1e-5
LAMBDA_INIT = 0.8 - 0.6 * math.exp(-0.3 * 0)

LANES = 128
PROJ_TILE = 1024
PROJ_ROWS = 512
GLA_TILE = 256
DIFF_TQ = 512
DIFF_SUB = DIFF_TQ // 2
DIFF_MAX_GAP = 84
TAIL_TM = 512
FFN_CHUNK = 256
VMEM_LIMIT = 56 * 1024 * 1024

_N_PROJ_TILES = 8
_T_QK, _T_GV, _T_GR, _T_DQ, _T_DK, _T_DV, _T_GA, _T_GB = range(_N_PROJ_TILES)

NT_DIMS = (((1,), (1,)), ((), ()))
TN_DIMS = (((0,), (0,)), ((), ()))


def _rms(x, w, eps):
    ms = jnp.mean(x * x, axis=-1, keepdims=True)
    return x * lax.rsqrt(ms + eps) * w


def _sigmoid(x):
    return 0.5 * jnp.tanh(0.5 * x) + 0.5


def _inproj_kernel(x_ref, nw_ref, w_ref, wglr_ref, wgk_ref, bgk_ref, cos_ref, sin_ref,
                   out_ref, g_ref):
    u = _rms(x_ref[...], nw_ref[...], NORM_EPS).astype(BF16)
    cos = cos_ref[...]
    sin = sin_ref[...]
    rope_scale = {_T_DQ: (DIFF_HD ** -0.5) * math.log2(math.e), _T_DK: 1.0}

    def gate():
        glr = jnp.dot(u, wglr_ref[...], preferred_element_type=F32).astype(BF16)
        logits = jnp.dot(glr, wgk_ref[...], preferred_element_type=F32) + bgk_ref[...]
        soft = jnp.log2(1.0 + jnp.exp2(jnp.abs(logits) * (-math.log2(math.e))))
        g_ref[...] = (jnp.minimum(logits, 0.0) * (1.0 / GLA_GATE_NORM)
                      - soft * (math.log(2.0) / GLA_GATE_NORM))

    for j in range(_N_PROJ_TILES):
        cols = slice(j * PROJ_TILE, (j + 1) * PROJ_TILE)
        acc = jnp.dot(u, w_ref[:, cols], preferred_element_type=F32)
        if j == _T_QK:
            out_ref[:, :GLA_K] = (acc[:, :GLA_K] * (GLA_DK ** -0.5)).astype(BF16)
            out_ref[:, GLA_K:PROJ_TILE] = acc[:, GLA_K:].astype(BF16)
        elif j in rope_scale:
            for g in range(PROJ_TILE // LANES):
                xg = acc[:, g * LANES:(g + 1) * LANES]
                y = xg * cos + pltpu.roll(xg, LANES // 2, axis=1) * sin
                if rope_scale[j] != 1.0:
                    y = y * rope_scale[j]
                out_ref[:, j * PROJ_TILE + g * LANES:j * PROJ_TILE + (g + 1) * LANES] = y.astype(BF16)
        else:
            out_ref[:, cols] = acc.astype(BF16)
        if j == 1:
            gate()


def _inproj(x2, nw, w_main, w_glr, wgk, bgk, cos, sin, seq):
    t = x2.shape[0]
    width = w_main.shape[1]
    assert width == _N_PROJ_TILES * PROJ_TILE
    pos_blocks = seq // PROJ_ROWS
    resident = pl.Buffered(1)
    return pl.pallas_call(
        _inproj_kernel,
        grid=(t // PROJ_ROWS,),
        in_specs=[
            pl.BlockSpec((PROJ_ROWS, D_MODEL), lambda i: (i, 0)),
            pl.BlockSpec((1, D_MODEL), lambda i: (0, 0)),
            pl.BlockSpec((D_MODEL, width), lambda i: (0, 0), pipeline_mode=resident),
            pl.BlockSpec((D_MODEL, LANES), lambda i: (0, 0), pipeline_mode=resident),
            pl.BlockSpec((LANES, 2 * GLA_K), lambda i: (0, 0), pipeline_mode=resident),
            pl.BlockSpec((1, 2 * GLA_K), lambda i: (0, 0)),
            pl.BlockSpec((PROJ_ROWS, LANES), lambda i: (i % pos_blocks, 0)),
            pl.BlockSpec((PROJ_ROWS, LANES), lambda i: (i % pos_blocks, 0)),
        ],
        out_specs=[
            pl.BlockSpec((PROJ_ROWS, width), lambda i: (i, 0)),
            pl.BlockSpec((PROJ_ROWS, 2 * GLA_K), lambda i: (i, 0)),
        ],
        out_shape=[
            jax.ShapeDtypeStruct((t, width), BF16),
            jax.ShapeDtypeStruct((t, 2 * GLA_K), F32),
        ],
        compiler_params=pltpu.CompilerParams(
            dimension_semantics=("parallel",), vmem_limit_bytes=VMEM_LIMIT),
        name="inproj",
    )(x2, nw, w_main, w_glr, wgk, bgk, cos, sin)


def _gla_kernel(q_ref, k_ref, v_ref, gr_ref, ga_ref, gf_ref, gb_ref, nw_ref, tri_ref, out_ref,
                qt_scr, kt_scr, ks_scr, dec_scr, of_scr, ob_scr, st_scr):
    seq = q_ref.shape[0]
    n_tiles = seq // GLA_TILE
    per_tile = GLA_TILE // GLA_CHUNK
    g_refs = (gf_ref, gb_ref)
    o_scrs = (of_scr, ob_scr)

    def decays(r, carry):
        r0 = pl.multiple_of(r * GLA_TILE, GLA_TILE)
        rows = pl.ds(r0, GLA_TILE)
        q = q_ref[rows, :].astype(F32)
        k = k_ref[rows, :].astype(F32)
        for d in range(2):
            g = g_refs[d][rows, :]
            hi = g.astype(BF16)
            lo = (g - hi.astype(F32)).astype(BF16)
            c2 = jnp.dot(tri_ref[d], jnp.concatenate([hi, lo], axis=1), preferred_element_type=F32)
            bcum = c2[:, :LANES] + c2[:, LANES:]
            edge = GLA_CHUNK - 1 if d == 0 else 0
            tots = [bcum[c * GLA_CHUNK + edge:c * GLA_CHUNK + edge + 1, :] for c in range(per_tile)]
            tot = jnp.concatenate([jnp.broadcast_to(tc, (GLA_CHUNK, GLA_DK)) for tc in tots], axis=0)
            qt_scr[d, rows, :] = (q * jnp.exp(bcum)).astype(BF16)
            kt_scr[d, rows, :] = (k * jnp.exp(-bcum)).astype(BF16)
            ks_scr[d, rows, :] = (k * jnp.exp(tot - bcum)).astype(BF16)
            for c in range(per_tile):
                dec_scr[d, r, pl.ds(c, 1), :] = jnp.exp(tots[c])
        return carry

    lax.fori_loop(0, n_tiles, decays, 0, unroll=4)

    st_scr[...] = jnp.zeros(st_scr.shape, F32)

    def scan(n, carry):
        for d in range(2):
            t = n if d == 0 else n_tiles - 1 - n
            base = pl.multiple_of(t * GLA_TILE, GLA_TILE)
            rows = pl.ds(base, GLA_TILE)
            qt = qt_scr[d, rows, :]
            ks = ks_scr[d, rows, :]
            v = v_ref[rows, :]
            a = lax.dot_general(qt, kt_scr[d, rows, :], NT_DIMS, preferred_element_type=F32)
            a = jnp.where(tri_ref[d] > 0, a, 0.0).astype(BF16)
            o_intra = jnp.dot(a, v, preferred_element_type=F32)
            st = st_scr[d]
            pieces = [None] * per_tile
            for c in (range(per_tile) if d == 0 else range(per_tile - 1, -1, -1)):
                sl = slice(c * GLA_CHUNK, (c + 1) * GLA_CHUNK)
                pieces[c] = o_intra[sl] + lax.dot_general(
                    qt[sl], st.astype(BF16), NT_DIMS, preferred_element_type=F32)
                kv_t = lax.dot_general(v[sl], ks[sl], TN_DIMS, preferred_element_type=F32)
                st = st * dec_scr[d, t, pl.ds(c, 1), :] + kv_t
            o_scrs[d][rows, :] = jnp.concatenate(pieces, axis=0)
            st_scr[d] = st
        return carry

    lax.fori_loop(0, n_tiles, scan, 0, unroll=8)

    def finish(r, carry):
        r0 = pl.multiple_of(r * GLA_TILE, GLA_TILE)
        rows = pl.ds(r0, GLA_TILE)
        y = _rms(of_scr[rows, :] + ob_scr[rows, :], nw_ref[...], NORM_EPS)
        gr = gr_ref[rows, :].astype(F32)
        ga = ga_ref[rows, :].astype(F32)
        out_ref[rows, :] = (_sigmoid(ga) * (y * (gr * _sigmoid(gr)))).astype(BF16)
        return carry

    lax.fori_loop(0, n_tiles, finish, 0, unroll=2)


def _gla(proj, g, nw, tri, batch, seq):
    t = proj.shape[0]
    kb = GLA_K // GLA_DK
    vb = PROJ_TILE // GLA_DV
    return pl.pallas_call(
        _gla_kernel,
        grid=(batch, GLA_HEADS),
        in_specs=[
            pl.BlockSpec((seq, GLA_DK), lambda b, h: (b, h)),
            pl.BlockSpec((seq, GLA_DK), lambda b, h: (b, kb + h)),
            pl.BlockSpec((seq, GLA_DV), lambda b, h: (b, _T_GV * vb + h)),
            pl.BlockSpec((seq, GLA_DV), lambda b, h: (b, _T_GR * vb + h)),
            pl.BlockSpec((seq, GLA_DV), lambda b, h: (b, _T_GA * vb + h)),
            pl.BlockSpec((seq, GLA_DK), lambda b, h: (b, h)),
            pl.BlockSpec((seq, GLA_DK), lambda b, h: (b, kb + h)),
            pl.BlockSpec((1, GLA_DV), lambda b, h: (0, 0)),
            pl.BlockSpec((2, GLA_TILE, GLA_TILE), lambda b, h: (0, 0, 0)),
        ],
        out_specs=pl.BlockSpec((seq, GLA_DV), lambda b, h: (b, h)),
        out_shape=jax.ShapeDtypeStruct((t, GLA_HEADS * GLA_DV), BF16),
        scratch_shapes=[
            pltpu.VMEM((2, seq, GLA_DK), BF16),
            pltpu.VMEM((2, seq, GLA_DK), BF16),
            pltpu.VMEM((2, seq, GLA_DK), BF16),
            pltpu.VMEM((2, seq // GLA_TILE, GLA_TILE // GLA_CHUNK, GLA_DK), F32),
            pltpu.VMEM((seq, GLA_DV), F32),
            pltpu.VMEM((seq, GLA_DV), F32),
            pltpu.VMEM((2, GLA_DV, GLA_DK), F32),
        ],
        compiler_params=pltpu.CompilerParams(
            dimension_semantics=("parallel", "parallel"), vmem_limit_bytes=VMEM_LIMIT),
        name="gla",
    )(proj, proj, proj, proj, proj, g, g, nw, tri)


def _diff_kernel(q_ref, k_ref, v_ref, gb_ref, za_ref, lam_ref, nw_ref, out_ref,
                 kt_scr, kn_scr, e_scr, l_scr):
    @pl.when(pl.program_id(2) == 0)
    def _():
        kt = k_ref[...].T
        extra = (lax.broadcasted_iota(jnp.int32, (DIFF_HD, kt.shape[1]), 0) == 0).astype(BF16)
        for c in range(2):
            kt_scr[c, :DIFF_HD, :] = kt[c * DIFF_HD:(c + 1) * DIFF_HD]
            kt_scr[c, DIFF_HD:, :] = extra
            kc = k_ref[:, c * DIFF_HD:(c + 1) * DIFF_HD].astype(F32)
            norm2 = jnp.max(jnp.sum(kc * kc, axis=-1, keepdims=True), axis=0, keepdims=True)
            kn_scr[c:c + 1, :] = jnp.broadcast_to(jnp.sqrt(norm2), (1, LANES))

    lv = lam_ref[...]
    lam = (jnp.exp(jnp.sum(lv[0:1] * lv[1:2], keepdims=True))
           - jnp.exp(jnp.sum(lv[2:3] * lv[3:4], keepdims=True)) + LAMBDA_INIT)

    first_lane = lax.broadcasted_iota(jnp.int32, (DIFF_TQ, DIFF_HD), 1) == 0
    smallest = []
    for c in range(2):
        qc = q_ref[:, c * DIFF_HD:(c + 1) * DIFF_HD]
        qf = qc.astype(F32)
        bound = jnp.sqrt(jnp.sum(qf * qf, axis=-1, keepdims=True)) * kn_scr[c:c + 1, 0:1]
        shift = jnp.where(first_lane, -bound, 0.0).astype(BF16)
        s = jnp.dot(jnp.concatenate([qc, shift], axis=1), kt_scr[c], preferred_element_type=F32)
        e = jnp.exp2(s)
        l = jnp.sum(e, axis=-1, keepdims=True)
        l_scr[c] = l
        e_scr[c] = e.astype(BF16)
        smallest.append(jnp.min(l))
    safe = jnp.minimum(smallest[0], smallest[1]) > 2.0 ** -DIFF_MAX_GAP

    @pl.when(jnp.logical_not(safe))
    def _():
        for c in range(2):
            qc = q_ref[:, c * DIFF_HD:(c + 1) * DIFF_HD]
            s = jnp.dot(qc, kt_scr[c, :DIFF_HD, :], preferred_element_type=F32)
            e = jnp.exp2(s - jnp.max(s, axis=-1, keepdims=True))
            l_scr[c] = jnp.sum(e, axis=-1, keepdims=True)
            e_scr[c] = e.astype(BF16)

    v = v_ref[...]
    w = e_scr[0] * (1.0 / l_scr[0]).astype(BF16) - e_scr[1] * (lam / l_scr[1]).astype(BF16)
    o = jnp.concatenate(
        [jnp.dot(w[:DIFF_SUB], v, preferred_element_type=F32),
         jnp.dot(w[DIFF_SUB:], v, preferred_element_type=F32)], axis=0)
    y = _rms(o, nw_ref[...], SUBLN_EPS) * (1.0 - LAMBDA_INIT)
    gb = gb_ref[...].astype(F32)
    out_ref[...] = (za_ref[...].astype(F32) + _sigmoid(gb) * y).astype(BF16)


def _diff(proj, za, lam4, nw, batch, seq):
    t = proj.shape[0]
    hw = 2 * DIFF_HD
    vb = PROJ_TILE // hw
    nq = seq // DIFF_TQ
    return pl.pallas_call(
        _diff_kernel,
        grid=(batch, DIFF_HEADS, nq),
        in_specs=[
            pl.BlockSpec((DIFF_TQ, hw), lambda b, h, i: (b * nq + i, _T_DQ * vb + h)),
            pl.BlockSpec((seq, hw), lambda b, h, i: (b, _T_DK * vb + h)),
            pl.BlockSpec((seq, hw), lambda b, h, i: (b, _T_DV * vb + h)),
            pl.BlockSpec((DIFF_TQ, hw), lambda b, h, i: (b * nq + i, _T_GB * vb + h)),
            pl.BlockSpec((DIFF_TQ, hw), lambda b, h, i: (b * nq + i, h)),
            pl.BlockSpec((4, DIFF_HD), lambda b, h, i: (0, 0)),
            pl.BlockSpec((1, hw), lambda b, h, i: (0, 0)),
        ],
        out_specs=pl.BlockSpec((DIFF_TQ, hw), lambda b, h, i: (b * nq + i, h)),
        out_shape=jax.ShapeDtypeStruct((t, DIFF_HEADS * hw), BF16),
        scratch_shapes=[
            pltpu.VMEM((2, hw, seq), BF16),
            pltpu.VMEM((8, LANES), F32),
            pltpu.VMEM((2, DIFF_TQ, seq), BF16),
            pltpu.VMEM((2, DIFF_TQ, 1), F32),
        ],
        compiler_params=pltpu.CompilerParams(
            dimension_semantics=("parallel", "parallel", "arbitrary"), vmem_limit_bytes=VMEM_LIMIT),
        name="diffattn",
    )(proj, proj, proj, proj, za, lam4, nw)


def _tail_kernel(m_ref, x_ref, wo_ref, nfw_ref, wi_ref, wd_ref, nlw_ref, out_ref):
    h1 = x_ref[...] + jnp.dot(m_ref[...], wo_ref[...], preferred_element_type=F32)
    n = _rms(h1, nfw_ref[...], NORM_EPS).astype(BF16)
    acc = h1
    for c in range(FFN_HIDDEN // FFN_CHUNK):
        cols = slice(c * FFN_CHUNK, (c + 1) * FFN_CHUNK)
        up_cols = slice(FFN_HIDDEN + c * FFN_CHUNK, FFN_HIDDEN + (c + 1) * FFN_CHUNK)
        g = jnp.dot(n, wi_ref[:, cols], preferred_element_type=F32)
        u = jnp.dot(n, wi_ref[:, up_cols], preferred_element_type=F32)
        hd = ((g * _sigmoid(g)) * u).astype(BF16)
        acc = acc + jnp.dot(hd, wd_ref[cols, :], preferred_element_type=F32)
    out_ref[...] = _rms(acc, nlw_ref[...], NORM_EPS)


def _tail(merged, x2, wo, nfw, wi, wd, nlw):
    t = x2.shape[0]
    resident = pl.Buffered(1)
    return pl.pallas_call(
        _tail_kernel,
        grid=(t // TAIL_TM,),
        in_specs=[
            pl.BlockSpec((TAIL_TM, D_MODEL), lambda i: (i, 0)),
            pl.BlockSpec((TAIL_TM, D_MODEL), lambda i: (i, 0)),
            pl.BlockSpec((D_MODEL, D_MODEL), lambda i: (0, 0), pipeline_mode=resident),
            pl.BlockSpec((1, D_MODEL), lambda i: (0, 0)),
            pl.BlockSpec((D_MODEL, 2 * FFN_HIDDEN), lambda i: (0, 0), pipeline_mode=resident),
            pl.BlockSpec((FFN_HIDDEN, D_MODEL), lambda i: (0, 0), pipeline_mode=resident),
            pl.BlockSpec((1, D_MODEL), lambda i: (0, 0)),
        ],
        out_specs=pl.BlockSpec((TAIL_TM, D_MODEL), lambda i: (i, 0)),
        out_shape=jax.ShapeDtypeStruct((t, D_MODEL), F32),
        compiler_params=pltpu.CompilerParams(
            dimension_semantics=("parallel",), vmem_limit_bytes=VMEM_LIMIT),
        name="tail",
    )(merged, x2, wo, nfw, wi, wd, nlw)


def _rope_tables(seq):
    inv_freq = 1.0 / (ROPE_THETA ** (np.arange(0, DIFF_HD, 2, dtype=np.float32) / DIFF_HD))
    freqs = np.arange(seq, dtype=np.float32)[:, None] * inv_freq[None, :].astype(np.float32)
    emb = np.concatenate([freqs, freqs], axis=-1)
    half_sign = np.concatenate([-np.ones(DIFF_HD // 2, np.float32), np.ones(DIFF_HD // 2, np.float32)])
    return (jnp.asarray(np.cos(emb), dtype=F32),
            jnp.asarray(np.sin(emb) * half_sign[None, :], dtype=F32))


def _chunk_masks():
    r = np.arange(GLA_TILE)
    same = (r[:, None] // GLA_CHUNK) == (r[None, :] // GLA_CHUNK)
    lower = same & (r[None, :] <= r[:, None])
    upper = same & (r[None, :] >= r[:, None])
    return jnp.asarray(np.stack([lower, upper]), dtype=BF16)


def kernel(x, norm_mix_w, w_in, w_gk2, b_gk, gla_norm_w, lambda_q1, lambda_k1, lambda_q2,
           lambda_k2, diff_subln_w, w_out, norm_ffn_w, w_ffn_in, w_ffn_out, norm_final_w):
    batch, seq, d = x.shape
    assert d == D_MODEL and seq % PROJ_TILE == 0 and w_in.shape[0] == 1
    x2 = x.reshape(batch * seq, d)

    w = w_in[0]
    o_glr = 2 * GLA_K + 2 * D_MODEL
    w_main = jnp.concatenate([w[:, :o_glr], w[:, o_glr + 2 * GLA_RANK:]], axis=1).astype(BF16)
    w_glr = jnp.pad(w[:, o_glr:o_glr + 2 * GLA_RANK], ((0, 0), (0, LANES - 2 * GLA_RANK))).astype(BF16)
    wgk = jnp.zeros((LANES, 2 * GLA_K), F32)
    wgk = (wgk.at[:GLA_RANK, :GLA_K].set(w_gk2[0, 0])
           .at[GLA_RANK:2 * GLA_RANK, GLA_K:].set(w_gk2[0, 1]).astype(BF16))
    bgk = b_gk[0].reshape(1, 2 * GLA_K)
    cos, sin = _rope_tables(seq)
    tri = _chunk_masks()
    lam4 = jnp.stack([lambda_q1[0], lambda_k1[0], lambda_q2[0], lambda_k2[0]])

    proj, g = _inproj(x2, norm_mix_w, w_main, w_glr, wgk, bgk, cos, sin, seq)
    za = _gla(proj, g, gla_norm_w, tri, batch, seq)
    merged = _diff(proj, za, lam4, diff_subln_w, batch, seq)
    out = _tail(merged, x2, w_out[0].astype(BF16), norm_ffn_w, w_ffn_in[0].astype(BF16),
                w_ffn_out[0].astype(BF16), norm_final_w.reshape(1, d))
    return out.reshape(batch, seq, d)
```

```python
import functools
import math

import jax
import jax.numpy as jnp
import numpy as np
from jax import lax
from jax.experimental import pallas as pl
from jax.experimental.pallas import tpu as pltpu

F32 = jnp.float32
BF16 = jnp.bfloat16

D_MODEL = 1024
GLA_HEADS = 4
GLA_K = 512
GLA_DK = 128
GLA_DV = 256
GLA_RANK = 16
GLA_GATE_NORM = 16.0
GLA_CHUNK = 64
DIFF_HEADS = 4
DIFF_HD = 128
ROPE_THETA = 10000.0
FFN_HIDDEN = 2816
NORM_EPS = 1e-6
SUBLN_EPS = 1e-5
LAMBDA_INIT = 0.8 - 0.6 * math.exp(-0.3 * 0)

LANES = 128
PROJ_TILE = 1024
PROJ_ROWS = 512
GLA_TILE = 256
DIFF_TQ = 1024
DIFF_ROWS = 512
DIFF_SUB = DIFF_ROWS // 2
DIFF_MAX_GAP = 84
TAIL_TM = 512
FFN_CHUNK = 256
VMEM_LIMIT = 56 * 1024 * 1024

_N_PROJ_TILES = 8
_T_QK, _T_GV, _T_GR, _T_DQ, _T_DK, _T_DV, _T_GA, _T_GB = range(_N_PROJ_TILES)

NT_DIMS = (((1,), (1,)), ((), ()))
TN_DIMS = (((0,), (0,)), ((), ()))


def _rms(x, w, eps):
    ms = jnp.mean(x * x, axis=-1, keepdims=True)
    return x * lax.rsqrt(ms + eps) * w


def _sigmoid(x):
    return 0.5 * jnp.tanh(0.5 * x) + 0.5


def _inproj_kernel(x_ref, nw_ref, w_ref, wglr_ref, wgk_ref, bgk_ref, cos_ref, sin_ref,
                   out_ref, g_ref):
    u = _rms(x_ref[...], nw_ref[...], NORM_EPS).astype(BF16)
    cos = cos_ref[...]
    sin = sin_ref[...]
    rope_scale = {_T_DQ: (DIFF_HD ** -0.5) * math.log2(math.e), _T_DK: 1.0}

    def gate():
        glr = jnp.dot(u, wglr_ref[...], preferred_element_type=F32).astype(BF16)
        logits = jnp.dot(glr, wgk_ref[...], preferred_element_type=F32) + bgk_ref[...]
        soft = jnp.log2(1.0 + jnp.exp2(jnp.abs(logits) * (-math.log2(math.e))))
        g_ref[...] = (jnp.minimum(logits, 0.0) * (1.0 / GLA_GATE_NORM)
                      - soft * (math.log(2.0) / GLA_GATE_NORM))

    for j in range(_N_PROJ_TILES):
        cols = slice(j * PROJ_TILE, (j + 1) * PROJ_TILE)
        acc = jnp.dot(u, w_ref[:, cols], preferred_element_type=F32)
        if j == _T_QK:
            out_ref[:, :GLA_K] = (acc[:, :GLA_K] * (GLA_DK ** -0.5)).astype(BF16)
            out_ref[:, GLA_K:PROJ_TILE] = acc[:, GLA_K:].astype(BF16)
        elif j in rope_scale:
            for g in range(PROJ_TILE // LANES):
                xg = acc[:, g * LANES:(g + 1) * LANES]
                y = xg * cos + pltpu.roll(xg, LANES // 2, axis=1) * sin
                if rope_scale[j] != 1.0:
                    y = y * rope_scale[j]
                out_ref[:, j * PROJ_TILE + g * LANES:j * PROJ_TILE + (g + 1) * LANES] = y.astype(BF16)
        else:
            out_ref[:, cols] = acc.astype(BF16)
        if j == 1:
            gate()


def _inproj(x2, nw, w_main, w_glr, wgk, bgk, cos, sin, seq):
    t = x2.shape[0]
    width = w_main.shape[1]
    assert width == _N_PROJ_TILES * PROJ_TILE
    pos_blocks = seq // PROJ_ROWS
    resident = pl.Buffered(1)
    return pl.pallas_call(
        _inproj_kernel,
        grid=(t // PROJ_ROWS,),
        in_specs=[
            pl.BlockSpec((PROJ_ROWS, D_MODEL), lambda i: (i, 0)),
            pl.BlockSpec((1, D_MODEL), lambda i: (0, 0)),
            pl.BlockSpec((D_MODEL, width), lambda i: (0, 0), pipeline_mode=resident),
            pl.BlockSpec((D_MODEL, LANES), lambda i: (0, 0), pipeline_mode=resident),
            pl.BlockSpec((LANES, 2 * GLA_K), lambda i: (0, 0), pipeline_mode=resident),
            pl.BlockSpec((1, 2 * GLA_K), lambda i: (0, 0)),
            pl.BlockSpec((PROJ_ROWS, LANES), lambda i: (i % pos_blocks, 0)),
            pl.BlockSpec((PROJ_ROWS, LANES), lambda i: (i % pos_blocks, 0)),
        ],
        out_specs=[
            pl.BlockSpec((PROJ_ROWS, width), lambda i: (i, 0)),
            pl.BlockSpec((PROJ_ROWS, 2 * GLA_K), lambda i: (i, 0)),
        ],
        out_shape=[
            jax.ShapeDtypeStruct((t, width), BF16),
            jax.ShapeDtypeStruct((t, 2 * GLA_K), F32),
        ],
        compiler_params=pltpu.CompilerParams(
            dimension_semantics=("parallel",), vmem_limit_bytes=VMEM_LIMIT),
        name="inproj",
    )(x2, nw, w_main, w_glr, wgk, bgk, cos, sin)


def _gla_kernel(q_ref, k_ref, v_ref, gr_ref, ga_ref, gf_ref, gb_ref, nw_ref, tri_ref, out_ref,
                qt_scr, kt_scr, ks_scr, dec_scr, of_scr, ob_scr, st_scr):
    seq = q_ref.shape[0]
    n_tiles = seq // GLA_TILE
    per_tile = GLA_TILE // GLA_CHUNK
    g_refs = (gf_ref, gb_ref)
    o_scrs = (of_scr, ob_scr)

    def decays(r, carry):
        r0 = pl.multiple_of(r * GLA_TILE, GLA_TILE)
        rows = pl.ds(r0, GLA_TILE)
        q = q_ref[rows, :].astype(F32)
        k = k_ref[rows, :].astype(F32)
        for d in range(2):
            g = g_refs[d][rows, :]
            hi = g.astype(BF16)
            lo = (g - hi.astype(F32)).astype(BF16)
            c2 = jnp.dot(tri_ref[d], jnp.concatenate([hi, lo], axis=1), preferred_element_type=F32)
            bcum = c2[:, :LANES] + c2[:, LANES:]
            edge = GLA_CHUNK - 1 if d == 0 else 0
            tots = [bcum[c * GLA_CHUNK + edge:c * GLA_CHUNK + edge + 1, :] for c in range(per_tile)]
            tot = jnp.concatenate([jnp.broadcast_to(tc, (GLA_CHUNK, GLA_DK)) for tc in tots], axis=0)
            qt_scr[d, rows, :] = (q * jnp.exp(bcum)).astype(BF16)
            kt_scr[d, rows, :] = (k * jnp.exp(-bcum)).astype(BF16)
            ks_scr[d, rows, :] = (k * jnp.exp(tot - bcum)).astype(BF16)
            for c in range(per_tile):
                dec_scr[d, r, pl.ds(c, 1), :] = jnp.exp(tots[c])
        return carry

    lax.fori_loop(0, n_tiles, decays, 0, unroll=4)

    st_scr[...] = jnp.zeros(st_scr.shape, F32)

    def scan(n, carry):
        for d in range(2):
            t = n if d == 0 else n_tiles - 1 - n
            base = pl.multiple_of(t * GLA_TILE, GLA_TILE)
            rows = pl.ds(base, GLA_TILE)
            qt = qt_scr[d, rows, :]
            ks = ks_scr[d, rows, :]
            v = v_ref[rows, :]
            a = lax.dot_general(qt, kt_scr[d, rows, :], NT_DIMS, preferred_element_type=F32)
            a = jnp.where(tri_ref[d] > 0, a, 0.0).astype(BF16)
            o_intra = jnp.dot(a, v, preferred_element_type=F32)
            st = st_scr[d]
            pieces = [None] * per_tile
            for c in (range(per_tile) if d == 0 else range(per_tile - 1, -1, -1)):
                sl = slice(c * GLA_CHUNK, (c + 1) * GLA_CHUNK)
                pieces[c] = o_intra[sl] + lax.dot_general(
                    qt[sl], st.astype(BF16), NT_DIMS, preferred_element_type=F32)
                kv_t = lax.dot_general(v[sl], ks[sl], TN_DIMS, preferred_element_type=F32)
                st = st * dec_scr[d, t, pl.ds(c, 1), :] + kv_t
            o_scrs[d][rows, :] = jnp.concatenate(pieces, axis=0)
            st_scr[d] = st
        return carry

    lax.fori_loop(0, n_tiles, scan, 0, unroll=8)

    def finish(r, carry):
        r0 = pl.multiple_of(r * GLA_TILE, GLA_TILE)
        rows = pl.ds(r0, GLA_TILE)
        y = _rms(of_scr[rows, :] + ob_scr[rows, :], nw_ref[...], NORM_EPS)
        gr = gr_ref[rows, :].astype(F32)
        ga = ga_ref[rows, :].astype(F32)
        out_ref[rows, :] = (_sigmoid(ga) * (y * (gr * _sigmoid(gr)))).astype(BF16)
        return carry

    lax.fori_loop(0, n_tiles, finish, 0, unroll=2)


def _gla(proj, g, nw, tri, batch, seq):
    t = proj.shape[0]
    kb = GLA_K // GLA_DK
    vb = PROJ_TILE // GLA_DV
    return pl.pallas_call(
        _gla_kernel,
        grid=(batch, GLA_HEADS),
        in_specs=[
            pl.BlockSpec((seq, GLA_DK), lambda b, h: (b, h)),
            pl.BlockSpec((seq, GLA_DK), lambda b, h: (b, kb + h)),
            pl.BlockSpec((seq, GLA_DV), lambda b, h: (b, _T_GV * vb + h)),
            pl.BlockSpec((seq, GLA_DV), lambda b, h: (b, _T_GR * vb + h)),
            pl.BlockSpec((seq, GLA_DV), lambda b, h: (b, _T_GA * vb + h)),
            pl.BlockSpec((seq, GLA_DK), lambda b, h: (b, h)),
            pl.BlockSpec((seq, GLA_DK), lambda b, h: (b, kb + h)),
            pl.BlockSpec((1, GLA_DV), lambda b, h: (0, 0)),
            pl.BlockSpec((2, GLA_TILE, GLA_TILE), lambda b, h: (0, 0, 0)),
        ],
        out_specs=pl.BlockSpec((seq, GLA_DV), lambda b, h: (b, h)),
        out_shape=jax.ShapeDtypeStruct((t, GLA_HEADS * GLA_DV), BF16),
        scratch_shapes=[
            pltpu.VMEM((2, seq, GLA_DK), BF16),
            pltpu.VMEM((2, seq, GLA_DK), BF16),
            pltpu.VMEM((2, seq, GLA_DK), BF16),
            pltpu.VMEM((2, seq // GLA_TILE, GLA_TILE // GLA_CHUNK, GLA_DK), F32),
            pltpu.VMEM((seq, GLA_DV), F32),
            pltpu.VMEM((seq, GLA_DV), F32),
            pltpu.VMEM((2, GLA_DV, GLA_DK), F32),
        ],
        compiler_params=pltpu.CompilerParams(
            dimension_semantics=("parallel", "parallel"), vmem_limit_bytes=VMEM_LIMIT),
        name="gla",
    )(proj, proj, proj, proj, proj, g, g, nw, tri)


def _diff_kernel(q_ref, k_ref, v_ref, gb_ref, za_ref, lam_ref, nw_ref, out_ref,
                 kt_scr, kn_scr, e_scr, l_scr):
    @pl.when(pl.program_id(2) == 0)
    def _():
        kt = k_ref[...].T
        extra = (lax.broadcasted_iota(jnp.int32, (DIFF_HD, kt.shape[1]), 0) == 0).astype(BF16)
        for c in range(2):
            kt_scr[c, :DIFF_HD, :] = kt[c * DIFF_HD:(c + 1) * DIFF_HD]
            kt_scr[c, DIFF_HD:, :] = extra
            kc = k_ref[:, c * DIFF_HD:(c + 1) * DIFF_HD].astype(F32)
            norm2 = jnp.max(jnp.sum(kc * kc, axis=-1, keepdims=True), axis=0, keepdims=True)
            kn_scr[c:c + 1, :] = jnp.broadcast_to(jnp.sqrt(norm2), (1, LANES))

    lv = lam_ref[...]
    lam = (jnp.exp(jnp.sum(lv[0:1] * lv[1:2], keepdims=True))
           - jnp.exp(jnp.sum(lv[2:3] * lv[3:4], keepdims=True)) + LAMBDA_INIT)

    first_lane = lax.broadcasted_iota(jnp.int32, (DIFF_ROWS, DIFF_HD), 1) == 0

    def sub_tile(rows):
        smallest = []
        for c in range(2):
            qc = q_ref[rows, c * DIFF_HD:(c + 1) * DIFF_HD]
            qf = qc.astype(F32)
            bound = jnp.sqrt(jnp.sum(qf * qf, axis=-1, keepdims=True)) * kn_scr[c:c + 1, 0:1]
            shift = jnp.where(first_lane, -bound, 0.0).astype(BF16)
            s = jnp.dot(jnp.concatenate([qc, shift], axis=1), kt_scr[c], preferred_element_type=F32)
            e = jnp.exp2(s)
            l = jnp.sum(e, axis=-1, keepdims=True)
            l_scr[c] = l
            e_scr[c] = e.astype(BF16)
            smallest.append(jnp.min(l))
        safe = jnp.minimum(smallest[0], smallest[1]) > 2.0 ** -DIFF_MAX_GAP

        @pl.when(jnp.logical_not(safe))
        def _():
            for c in range(2):
                qc = q_ref[rows, c * DIFF_HD:(c + 1) * DIFF_HD]
                s = jnp.dot(qc, kt_scr[c, :DIFF_HD, :], preferred_element_type=F32)
                e = jnp.exp2(s - jnp.max(s, axis=-1, keepdims=True))
                l_scr[c] = jnp.sum(e, axis=-1, keepdims=True)
                e_scr[c] = e.astype(BF16)

        v = v_ref[...]
        w = e_scr[0] * (1.0 / l_scr[0]).astype(BF16) - e_scr[1] * (lam / l_scr[1]).astype(BF16)
        o = jnp.concatenate(
            [jnp.dot(w[:DIFF_SUB], v, preferred_element_type=F32),
             jnp.dot(w[DIFF_SUB:], v, preferred_element_type=F32)], axis=0)
        y = _rms(o, nw_ref[...], SUBLN_EPS) * (1.0 - LAMBDA_INIT)
        gb = gb_ref[rows, :].astype(F32)
        out_ref[rows, :] = (za_ref[rows, :].astype(F32) + _sigmoid(gb) * y).astype(BF16)

    for r in range(DIFF_TQ // DIFF_ROWS):
        sub_tile(slice(r * DIFF_ROWS, (r + 1) * DIFF_ROWS))


def _diff(proj, za, lam4, nw, batch, seq):
    t = proj.shape[0]
    hw = 2 * DIFF_HD
    vb = PROJ_TILE // hw
    nq = seq // DIFF_TQ
    return pl.pallas_call(
        _diff_kernel,
        grid=(batch, DIFF_HEADS, nq),
        in_specs=[
            pl.BlockSpec((DIFF_TQ, hw), lambda b, h, i: (b * nq + i, _T_DQ * vb + h)),
            pl.BlockSpec((seq, hw), lambda b, h, i: (b, _T_DK * vb + h)),
            pl.BlockSpec((seq, hw), lambda b, h, i: (b, _T_DV * vb + h)),
            pl.BlockSpec((DIFF_TQ, hw), lambda b, h, i: (b * nq + i, _T_GB * vb + h)),
            pl.BlockSpec((DIFF_TQ, hw), lambda b, h, i: (b * nq + i, h)),
            pl.BlockSpec((4, DIFF_HD), lambda b, h, i: (0, 0)),
            pl.BlockSpec((1, hw), lambda b, h, i: (0, 0)),
        ],
        out_specs=pl.BlockSpec((DIFF_TQ, hw), lambda b, h, i: (b * nq + i, h)),
        out_shape=jax.ShapeDtypeStruct((t, DIFF_HEADS * hw), BF16),
        scratch_shapes=[
            pltpu.VMEM((2, hw, seq), BF16),
            pltpu.VMEM((8, LANES), F32),
            pltpu.VMEM((2, DIFF_ROWS, seq), BF16),
            pltpu.VMEM((2, DIFF_ROWS, 1), F32),
        ],
        compiler_params=pltpu.CompilerParams(
            dimension_semantics=("parallel", "parallel", "arbitrary"), vmem_limit_bytes=VMEM_LIMIT),
        name="diffattn",
    )(proj, proj, proj, proj, za, lam4, nw)


def _tail_kernel(m_ref, x_ref, wo_ref, nfw_ref, wi_ref, wd_ref, nlw_ref, out_ref):
    h1 = x_ref[...] + jnp.dot(m_ref[...], wo_ref[...], preferred_element_type=F32)
    n = _rms(h1, nfw_ref[...], NORM_EPS).astype(BF16)
    acc = h1
    for c in range(FFN_HIDDEN // FFN_CHUNK):
        cols = slice(c * FFN_CHUNK, (c + 1) * FFN_CHUNK)
        up_cols = slice(FFN_HIDDEN + c * FFN_CHUNK, FFN_HIDDEN + (c + 1) * FFN_CHUNK)
        g = jnp.dot(n, wi_ref[:, cols], preferred_element_type=F32)
        u = jnp.dot(n, wi_ref[:, up_cols], preferred_element_type=F32)
        hd = ((g * _sigmoid(g)) * u).astype(BF16)
        acc = acc + jnp.dot(hd, wd_ref[cols, :], preferred_element_type=F32)
    out_ref[...] = _rms(acc, nlw_ref[...], NORM_EPS)


def _tail(merged, x2, wo, nfw, wi, wd, nlw):
    t = x2.shape[0]
    resident = pl.Buffered(1)
    return pl.pallas_call(
        _tail_kernel,
        grid=(t // TAIL_TM,),
        in_specs=[
            pl.BlockSpec((TAIL_TM, D_MODEL), lambda i: (i, 0)),
            pl.BlockSpec((TAIL_TM, D_MODEL), lambda i: (i, 0)),
            pl.BlockSpec((D_MODEL, D_MODEL), lambda i: (0, 0), pipeline_mode=resident),
            pl.BlockSpec((1, D_MODEL), lambda i: (0, 0)),
            pl.BlockSpec((D_MODEL, 2 * FFN_HIDDEN), lambda i: (0, 0), pipeline_mode=resident),
            pl.BlockSpec((FFN_HIDDEN, D_MODEL), lambda i: (0, 0), pipeline_mode=resident),
            pl.BlockSpec((1, D_MODEL), lambda i: (0, 0)),
        ],
        out_specs=pl.BlockSpec((TAIL_TM, D_MODEL), lambda i: (i, 0)),
        out_shape=jax.ShapeDtypeStruct((t, D_MODEL), F32),
        compiler_params=pltpu.CompilerParams(
            dimension_semantics=("parallel",), vmem_limit_bytes=VMEM_LIMIT),
        name="tail",
    )(merged, x2, wo, nfw, wi, wd, nlw)


def _rope_tables(seq):
    inv_freq = 1.0 / (ROPE_THETA ** (np.arange(0, DIFF_HD, 2, dtype=np.float32) / DIFF_HD))
    freqs = np.arange(seq, dtype=np.float32)[:, None] * inv_freq[None, :].astype(np.float32)
    emb = np.concatenate([freqs, freqs], axis=-1)
    half_sign = np.concatenate([-np.ones(DIFF_HD // 2, np.float32), np.ones(DIFF_HD // 2, np.float32)])
    return (jnp.asarray(np.cos(emb), dtype=F32),
            jnp.asarray(np.sin(emb) * half_sign[None, :], dtype=F32))


def _chunk_masks():
    r = np.arange(GLA_TILE)
    same = (r[:, None] // GLA_CHUNK) == (r[None, :] // GLA_CHUNK)
    lower = same & (r[None, :] <= r[:, None])
    upper = same & (r[None, :] >= r[:, None])
    return jnp.asarray(np.stack([lower, upper]), dtype=BF16)


def kernel(x, norm_mix_w, w_in, w_gk2, b_gk, gla_norm_w, lambda_q1, lambda_k1, lambda_q2,
           lambda_k2, diff_subln_w, w_out, norm_ffn_w, w_ffn_in, w_ffn_out, norm_final_w):
    batch, seq, d = x.shape
    assert d == D_MODEL and seq % PROJ_TILE == 0 and w_in.shape[0] == 1
    x2 = x.reshape(batch * seq, d)

    w = w_in[0]
    o_glr = 2 * GLA_K + 2 * D_MODEL
    w_main = jnp.concatenate([w[:, :o_glr], w[:, o_glr + 2 * GLA_RANK:]], axis=1).astype(BF16)
    w_glr = jnp.pad(w[:, o_glr:o_glr + 2 * GLA_RANK], ((0, 0), (0, LANES - 2 * GLA_RANK))).astype(BF16)
    wgk = jnp.zeros((LANES, 2 * GLA_K), F32)
    wgk = (wgk.at[:GLA_RANK, :GLA_K].set(w_gk2[0, 0])
           .at[GLA_RANK:2 * GLA_RANK, GLA_K:].set(w_gk2[0, 1]).astype(BF16))
    bgk = b_gk[0].reshape(1, 2 * GLA_K)
    cos, sin = _rope_tables(seq)
    tri = _chunk_masks()
    lam4 = jnp.stack([lambda_q1[0], lambda_k1[0], lambda_q2[0], lambda_k2[0]])

    proj, g = _inproj(x2, norm_mix_w, w_main, w_glr, wgk, bgk, cos, sin, seq)
    za = _gla(proj, g, gla_norm_w, tri, batch, seq)
    merged = _diff(proj, za, lam4, diff_subln_w, batch, seq)
    out = _tail(merged, x2, w_out[0].astype(BF16), norm_ffn_w, w_ffn_in[0].astype(BF16),
                w_ffn_out[0].astype(BF16), norm_final_w.reshape(1, d))
    return out.reshape(batch, seq, d)
```

```python
import functools
import math

import jax
import jax.numpy as jnp
import numpy as np
from jax import lax
from jax.experimental import pallas as pl
from jax.experimental.pallas import tpu as pltpu

F32 = jnp.float32
BF16 = jnp.bfloat16

D_MODEL = 1024
GLA_HEADS = 4
GLA_K = 512
GLA_DK = 128
GLA_DV = 256
GLA_RANK = 16
GLA_GATE_NORM = 16.0
GLA_CHUNK = 64
DIFF_HEADS = 4
DIFF_HD = 128
ROPE_THETA = 10000.0
FFN_HIDDEN = 2816
NORM_EPS = 1e-6
SUBLN_EPS = 1e-5
LAMBDA_INIT = 0.8 - 0.6 * math.exp(-0.3 * 0)

LANES = 128
PROJ_TILE = 1024
PROJ_ROWS = 512
GLA_TILE = 256
DIFF_TQ = 2048
DIFF_ROWS = 512
DIFF_SUB = DIFF_ROWS // 2
DIFF_MAX_GAP = 84
TAIL_TM = 512
FFN_CHUNK = 256
VMEM_LIMIT = 56 * 1024 * 1024

_N_PROJ_TILES = 8
_T_QK, _T_GV, _T_GR, _T_DQ, _T_DK, _T_DV, _T_GA, _T_GB = range(_N_PROJ_TILES)

NT_DIMS = (((1,), (1,)), ((), ()))
TN_DIMS = (((0,), (0,)), ((), ()))


def _rms(x, w, eps):
    ms = jnp.mean(x * x, axis=-1, keepdims=True)
    return x * lax.rsqrt(ms + eps) * w


def _sigmoid(x):
    return 0.5 * jnp.tanh(0.5 * x) + 0.5


def _inproj_kernel(x_ref, nw_ref, w_ref, wglr_ref, wgk_ref, bgk_ref, cos_ref, sin_ref,
                   out_ref, g_ref):
    u = _rms(x_ref[...], nw_ref[...], NORM_EPS).astype(BF16)
    cos = cos_ref[...]
    sin = sin_ref[...]
    rope_scale = {_T_DQ: (DIFF_HD ** -0.5) * math.log2(math.e), _T_DK: 1.0}

    def gate():
        glr = jnp.dot(u, wglr_ref[...], preferred_element_type=F32).astype(BF16)
        logits = jnp.dot(glr, wgk_ref[...], preferred_element_type=F32) + bgk_ref[...]
        soft = jnp.log2(1.0 + jnp.exp2(jnp.abs(logits) * (-math.log2(math.e))))
        g_ref[...] = (jnp.minimum(logits, 0.0) * (1.0 / GLA_GATE_NORM)
                      - soft * (math.log(2.0) / GLA_GATE_NORM))

    for j in range(_N_PROJ_TILES):
        cols = slice(j * PROJ_TILE, (j + 1) * PROJ_TILE)
        acc = jnp.dot(u, w_ref[:, cols], preferred_element_type=F32)
        if j == _T_QK:
            out_ref[:, :GLA_K] = (acc[:, :GLA_K] * (GLA_DK ** -0.5)).astype(BF16)
            out_ref[:, GLA_K:PROJ_TILE] = acc[:, GLA_K:].astype(BF16)
        elif j in rope_scale:
            for g in range(PROJ_TILE // LANES):
                xg = acc[:, g * LANES:(g + 1) * LANES]
                y = xg * cos + pltpu.roll(xg, LANES // 2, axis=1) * sin
                if rope_scale[j] != 1.0:
                    y = y * rope_scale[j]
                out_ref[:, j * PROJ_TILE + g * LANES:j * PROJ_TILE + (g + 1) * LANES] = y.astype(BF16)
        else:
            out_ref[:, cols] = acc.astype(BF16)
        if j == 1:
            gate()


def _inproj(x2, nw, w_main, w_glr, wgk, bgk, cos, sin, seq):
    t = x2.shape[0]
    width = w_main.shape[1]
    assert width == _N_PROJ_TILES * PROJ_TILE
    pos_blocks = seq // PROJ_ROWS
    resident = pl.Buffered(1)
    return pl.pallas_call(
        _inproj_kernel,
        grid=(t // PROJ_ROWS,),
        in_specs=[
            pl.BlockSpec((PROJ_ROWS, D_MODEL), lambda i: (i, 0)),
            pl.BlockSpec((1, D_MODEL), lambda i: (0, 0)),
            pl.BlockSpec((D_MODEL, width), lambda i: (0, 0), pipeline_mode=resident),
            pl.BlockSpec((D_MODEL, LANES), lambda i: (0, 0), pipeline_mode=resident),
            pl.BlockSpec((LANES, 2 * GLA_K), lambda i: (0, 0), pipeline_mode=resident),
            pl.BlockSpec((1, 2 * GLA_K), lambda i: (0, 0)),
            pl.BlockSpec((PROJ_ROWS, LANES), lambda i: (i % pos_blocks, 0)),
            pl.BlockSpec((PROJ_ROWS, LANES), lambda i: (i % pos_blocks, 0)),
        ],
        out_specs=[
            pl.BlockSpec((PROJ_ROWS, width), lambda i: (i, 0)),
            pl.BlockSpec((PROJ_ROWS, 2 * GLA_K), lambda i: (i, 0)),
        ],
        out_shape=[
            jax.ShapeDtypeStruct((t, width), BF16),
            jax.ShapeDtypeStruct((t, 2 * GLA_K), F32),
        ],
        compiler_params=pltpu.CompilerParams(
            dimension_semantics=("parallel",), vmem_limit_bytes=VMEM_LIMIT),
        name="inproj",
    )(x2, nw, w_main, w_glr, wgk, bgk, cos, sin)


def _gla_kernel(q_ref, k_ref, v_ref, gr_ref, ga_ref, gf_ref, gb_ref, nw_ref, tri_ref, out_ref,
                qt_scr, kt_scr, ks_scr, dec_scr, of_scr, ob_scr, st_scr):
    seq = q_ref.shape[0]
    n_tiles = seq // GLA_TILE
    per_tile = GLA_TILE // GLA_CHUNK
    g_refs = (gf_ref, gb_ref)
    o_scrs = (of_scr, ob_scr)

    def decays(r, carry):
        r0 = pl.multiple_of(r * GLA_TILE, GLA_TILE)
        rows = pl.ds(r0, GLA_TILE)
        q = q_ref[rows, :].astype(F32)
        k = k_ref[rows, :].astype(F32)
        for d in range(2):
            g = g_refs[d][rows, :]
            hi = g.astype(BF16)
            lo = (g - hi.astype(F32)).astype(BF16)
            c2 = jnp.dot(tri_ref[d], jnp.concatenate([hi, lo], axis=1), preferred_element_type=F32)
            bcum = c2[:, :LANES] + c2[:, LANES:]
            edge = GLA_CHUNK - 1 if d == 0 else 0
            tots = [bcum[c * GLA_CHUNK + edge:c * GLA_CHUNK + edge + 1, :] for c in range(per_tile)]
            tot = jnp.concatenate([jnp.broadcast_to(tc, (GLA_CHUNK, GLA_DK)) for tc in tots], axis=0)
            qt_scr[d, rows, :] = (q * jnp.exp(bcum)).astype(BF16)
            kt_scr[d, rows, :] = (k * jnp.exp(-bcum)).astype(BF16)
            ks_scr[d, rows, :] = (k * jnp.exp(tot - bcum)).astype(BF16)
            for c in range(per_tile):
                dec_scr[d, r, pl.ds(c, 1), :] = jnp.exp(tots[c])
        return carry

    lax.fori_loop(0, n_tiles, decays, 0, unroll=4)

    st_scr[...] = jnp.zeros(st_scr.shape, F32)

    def scan(n, carry):
        for d in range(2):
            t = n if d == 0 else n_tiles - 1 - n
            base = pl.multiple_of(t * GLA_TILE, GLA_TILE)
            rows = pl.ds(base, GLA_TILE)
            qt = qt_scr[d, rows, :]
            ks = ks_scr[d, rows, :]
            v = v_ref[rows, :]
            a = lax.dot_general(qt, kt_scr[d, rows, :], NT_DIMS, preferred_element_type=F32)
            a = jnp.where(tri_ref[d] > 0, a, 0.0).astype(BF16)
            o_intra = jnp.dot(a, v, preferred_element_type=F32)
            st = st_scr[d]
            pieces = [None] * per_tile
            for c in (range(per_tile) if d == 0 else range(per_tile - 1, -1, -1)):
                sl = slice(c * GLA_CHUNK, (c + 1) * GLA_CHUNK)
                pieces[c] = o_intra[sl] + lax.dot_general(
                    qt[sl], st.astype(BF16), NT_DIMS, preferred_element_type=F32)
                kv_t = lax.dot_general(v[sl], ks[sl], TN_DIMS, preferred_element_type=F32)
                st = st * dec_scr[d, t, pl.ds(c, 1), :] + kv_t
            o_scrs[d][rows, :] = jnp.concatenate(pieces, axis=0)
            st_scr[d] = st
        return carry

    lax.fori_loop(0, n_tiles, scan, 0, unroll=8)

    def finish(r, carry):
        r0 = pl.multiple_of(r * GLA_TILE, GLA_TILE)
        rows = pl.ds(r0, GLA_TILE)
        y = _rms(of_scr[rows, :] + ob_scr[rows, :], nw_ref[...], NORM_EPS)
        gr = gr_ref[rows, :].astype(F32)
        ga = ga_ref[rows, :].astype(F32)
        out_ref[rows, :] = (_sigmoid(ga) * (y * (gr * _sigmoid(gr)))).astype(BF16)
        return carry

    lax.fori_loop(0, n_tiles, finish, 0, unroll=2)


def _gla(proj, g, nw, tri, batch, seq):
    t = proj.shape[0]
    kb = GLA_K // GLA_DK
    vb = PROJ_TILE // GLA_DV
    return pl.pallas_call(
        _gla_kernel,
        grid=(batch, GLA_HEADS),
        in_specs=[
            pl.BlockSpec((seq, GLA_DK), lambda b, h: (b, h)),
            pl.BlockSpec((seq, GLA_DK), lambda b, h: (b, kb + h)),
            pl.BlockSpec((seq, GLA_DV), lambda b, h: (b, _T_GV * vb + h)),
            pl.BlockSpec((seq, GLA_DV), lambda b, h: (b, _T_GR * vb + h)),
            pl.BlockSpec((seq, GLA_DV), lambda b, h: (b, _T_GA * vb + h)),
            pl.BlockSpec((seq, GLA_DK), lambda b, h: (b, h)),
            pl.BlockSpec((seq, GLA_DK), lambda b, h: (b, kb + h)),
            pl.BlockSpec((1, GLA_DV), lambda b, h: (0, 0)),
            pl.BlockSpec((2, GLA_TILE, GLA_TILE), lambda b, h: (0, 0, 0)),
        ],
        out_specs=pl.BlockSpec((seq, GLA_DV), lambda b, h: (b, h)),
        out_shape=jax.ShapeDtypeStruct((t, GLA_HEADS * GLA_DV), BF16),
        scratch_shapes=[
            pltpu.VMEM((2, seq, GLA_DK), BF16),
            pltpu.VMEM((2, seq, GLA_DK), BF16),
            pltpu.VMEM((2, seq, GLA_DK), BF16),
            pltpu.VMEM((2, seq // GLA_TILE, GLA_TILE // GLA_CHUNK, GLA_DK), F32),
            pltpu.VMEM((seq, GLA_DV), F32),
            pltpu.VMEM((seq, GLA_DV), F32),
            pltpu.VMEM((2, GLA_DV, GLA_DK), F32),
        ],
        compiler_params=pltpu.CompilerParams(
            dimension_semantics=("parallel", "parallel"), vmem_limit_bytes=VMEM_LIMIT),
        name="gla",
    )(proj, proj, proj, proj, proj, g, g, nw, tri)


def _diff_kernel(q_ref, k_ref, v_ref, gb_ref, za_ref, lam_ref, nw_ref, out_ref,
                 kt_scr, kn_scr, e_scr, l_scr):
    @pl.when(pl.program_id(2) == 0)
    def _():
        kt = k_ref[...].T
        extra = (lax.broadcasted_iota(jnp.int32, (DIFF_HD, kt.shape[1]), 0) == 0).astype(BF16)
        for c in range(2):
            kt_scr[c, :DIFF_HD, :] = kt[c * DIFF_HD:(c + 1) * DIFF_HD]
            kt_scr[c, DIFF_HD:, :] = extra
            kc = k_ref[:, c * DIFF_HD:(c + 1) * DIFF_HD].astype(F32)
            norm2 = jnp.max(jnp.sum(kc * kc, axis=-1, keepdims=True), axis=0, keepdims=True)
            kn_scr[c:c + 1, :] = jnp.broadcast_to(jnp.sqrt(norm2), (1, LANES))

    lv = lam_ref[...]
    lam = (jnp.exp(jnp.sum(lv[0:1] * lv[1:2], keepdims=True))
           - jnp.exp(jnp.sum(lv[2:3] * lv[3:4], keepdims=True)) + LAMBDA_INIT)

    first_lane = lax.broadcasted_iota(jnp.int32, (DIFF_ROWS, DIFF_HD), 1) == 0

    def sub_tile(rows):
        smallest = []
        for c in range(2):
            qc = q_ref[rows, c * DIFF_HD:(c + 1) * DIFF_HD]
            qf = qc.astype(F32)
            bound = jnp.sqrt(jnp.sum(qf * qf, axis=-1, keepdims=True)) * kn_scr[c:c + 1, 0:1]
            shift = jnp.where(first_lane, -bound, 0.0).astype(BF16)
            s = jnp.dot(jnp.concatenate([qc, shift], axis=1), kt_scr[c], preferred_element_type=F32)
            e = jnp.exp2(s)
            l = jnp.sum(e, axis=-1, keepdims=True)
            l_scr[c] = l
            e_scr[c] = e.astype(BF16)
            smallest.append(jnp.min(l))
        safe = jnp.minimum(smallest[0], smallest[1]) > 2.0 ** -DIFF_MAX_GAP

        @pl.when(jnp.logical_not(safe))
        def _():
            for c in range(2):
                qc = q_ref[rows, c * DIFF_HD:(c + 1) * DIFF_HD]
                s = jnp.dot(qc, kt_scr[c, :DIFF_HD, :], preferred_element_type=F32)
                e = jnp.exp2(s - jnp.max(s, axis=-1, keepdims=True))
                l_scr[c] = jnp.sum(e, axis=-1, keepdims=True)
                e_scr[c] = e.astype(BF16)

        v = v_ref[...]
        w = e_scr[0] * (1.0 / l_scr[0]).astype(BF16) - e_scr[1] * (lam / l_scr[1]).astype(BF16)
        o = jnp.concatenate(
            [jnp.dot(w[:DIFF_SUB], v, preferred_element_type=F32),
             jnp.dot(w[DIFF_SUB:], v, preferred_element_type=F32)], axis=0)
        y = _rms(o, nw_ref[...], SUBLN_EPS) * (1.0 - LAMBDA_INIT)
        gb = gb_ref[rows, :].astype(F32)
        out_ref[rows, :] = (za_ref[rows, :].astype(F32) + _sigmoid(gb) * y).astype(BF16)

    for r in range(DIFF_TQ // DIFF_ROWS):
        sub_tile(slice(r * DIFF_ROWS, (r + 1) * DIFF_ROWS))


def _diff(proj, za, lam4, nw, batch, seq):
    t = proj.shape[0]
    hw = 2 * DIFF_HD
    vb = PROJ_TILE // hw
    nq = seq // DIFF_TQ
    return pl.pallas_call(
        _diff_kernel,
        grid=(batch, DIFF_HEADS, nq),
        in_specs=[
            pl.BlockSpec((DIFF_TQ, hw), lambda b, h, i: (b * nq + i, _T_DQ * vb + h)),
            pl.BlockSpec((seq, hw), lambda b, h, i: (b, _T_DK * vb + h)),
            pl.BlockSpec((seq, hw), lambda b, h, i: (b, _T_DV * vb + h)),
            pl.BlockSpec((DIFF_TQ, hw), lambda b, h, i: (b * nq + i, _T_GB * vb + h)),
            pl.BlockSpec((DIFF_TQ, hw), lambda b, h, i: (b * nq + i, h)),
            pl.BlockSpec((4, DIFF_HD), lambda b, h, i: (0, 0)),
            pl.BlockSpec((1, hw), lambda b, h, i: (0, 0)),
        ],
        out_specs=pl.BlockSpec((DIFF_TQ, hw), lambda b, h, i: (b * nq + i, h)),
        out_shape=jax.ShapeDtypeStruct((t, DIFF_HEADS * hw), BF16),
        scratch_shapes=[
            pltpu.VMEM((2, hw, seq), BF16),
            pltpu.VMEM((8, LANES), F32),
            pltpu.VMEM((2, DIFF_ROWS, seq), BF16),
            pltpu.VMEM((2, DIFF_ROWS, 1), F32),
        ],
        compiler_params=pltpu.CompilerParams(
            dimension_semantics=("parallel", "parallel", "arbitrary"), vmem_limit_bytes=VMEM_LIMIT),
        name="diffattn",
    )(proj, proj, proj, proj, za, lam4, nw)


def _tail_kernel(m_ref, x_ref, wo_ref, nfw_ref, wi_ref, wd_ref, nlw_ref, out_ref):
    h1 = x_ref[...] + jnp.dot(m_ref[...], wo_ref[...], preferred_element_type=F32)
    n = _rms(h1, nfw_ref[...], NORM_EPS).astype(BF16)
    acc = h1
    for c in range(FFN_HIDDEN // FFN_CHUNK):
        cols = slice(c * FFN_CHUNK, (c + 1) * FFN_CHUNK)
        up_cols = slice(FFN_HIDDEN + c * FFN_CHUNK, FFN_HIDDEN + (c + 1) * FFN_CHUNK)
        g = jnp.dot(n, wi_ref[:, cols], preferred_element_type=F32)
        u = jnp.dot(n, wi_ref[:, up_cols], preferred_element_type=F32)
        hd = ((g * _sigmoid(g)) * u).astype(BF16)
        acc = acc + jnp.dot(hd, wd_ref[cols, :], preferred_element_type=F32)
    out_ref[...] = _rms(acc, nlw_ref[...], NORM_EPS)


def _tail(merged, x2, wo, nfw, wi, wd, nlw):
    t = x2.shape[0]
    resident = pl.Buffered(1)
    return pl.pallas_call(
        _tail_kernel,
        grid=(t // TAIL_TM,),
        in_specs=[
            pl.BlockSpec((TAIL_TM, D_MODEL), lambda i: (i, 0)),
            pl.BlockSpec((TAIL_TM, D_MODEL), lambda i: (i, 0)),
            pl.BlockSpec((D_MODEL, D_MODEL), lambda i: (0, 0), pipeline_mode=resident),
            pl.BlockSpec((1, D_MODEL), lambda i: (0, 0)),
            pl.BlockSpec((D_MODEL, 2 * FFN_HIDDEN), lambda i: (0, 0), pipeline_mode=resident),
            pl.BlockSpec((FFN_HIDDEN, D_MODEL), lambda i: (0, 0), pipeline_mode=resident),
            pl.BlockSpec((1, D_MODEL), lambda i: (0, 0)),
        ],
        out_specs=pl.BlockSpec((TAIL_TM, D_MODEL), lambda i: (i, 0)),
        out_shape=jax.ShapeDtypeStruct((t, D_MODEL), F32),
        compiler_params=pltpu.CompilerParams(
            dimension_semantics=("parallel",), vmem_limit_bytes=VMEM_LIMIT),
        name="tail",
    )(merged, x2, wo, nfw, wi, wd, nlw)


def _rope_tables(seq):
    inv_freq = 1.0 / (ROPE_THETA ** (np.arange(0, DIFF_HD, 2, dtype=np.float32) / DIFF_HD))
    freqs = np.arange(seq, dtype=np.float32)[:, None] * inv_freq[None, :].astype(np.float32)
    emb = np.concatenate([freqs, freqs], axis=-1)
    half_sign = np.concatenate([-np.ones(DIFF_HD // 2, np.float32), np.ones(DIFF_HD // 2, np.float32)])
    return (jnp.asarray(np.cos(emb), dtype=F32),
            jnp.asarray(np.sin(emb) * half_sign[None, :], dtype=F32))


def _chunk_masks():
    r = np.arange(GLA_TILE)
    same = (r[:, None] // GLA_CHUNK) == (r[None, :] // GLA_CHUNK)
    lower = same & (r[None, :] <= r[:, None])
    upper = same & (r[None, :] >= r[:, None])
    return jnp.asarray(np.stack([lower, upper]), dtype=BF16)


def kernel(x, norm_mix_w, w_in, w_gk2, b_gk, gla_norm_w, lambda_q1, lambda_k1, lambda_q2,
           lambda_k2, diff_subln_w, w_out, norm_ffn_w, w_ffn_in, w_ffn_out, norm_final_w):
    batch, seq, d = x.shape
    assert d == D_MODEL and seq % PROJ_TILE == 0 and w_in.shape[0] == 1
    x2 = x.reshape(batch * seq, d)

    w = w_in[0]
    o_glr = 2 * GLA_K + 2 * D_MODEL
    w_main = jnp.concatenate([w[:, :o_glr], w[:, o_glr + 2 * GLA_RANK:]], axis=1).astype(BF16)
    w_glr = jnp.pad(w[:, o_glr:o_glr + 2 * GLA_RANK], ((0, 0), (0, LANES - 2 * GLA_RANK))).astype(BF16)
    wgk = jnp.zeros((LANES, 2 * GLA_K), F32)
    wgk = (wgk.at[:GLA_RANK, :GLA_K].set(w_gk2[0, 0])
           .at[GLA_RANK:2 * GLA_RANK, GLA_K:].set(w_gk2[0, 1]).astype(BF16))
    bgk = b_gk[0].reshape(1, 2 * GLA_K)
    cos, sin = _rope_tables(seq)
    tri = _chunk_masks()
    lam4 = jnp.stack([lambda_q1[0], lambda_k1[0], lambda_q2[0], lambda_k2[0]])

    proj, g = _inproj(x2, norm_mix_w, w_main, w_glr, wgk, bgk, cos, sin, seq)
    za = _gla(proj, g, gla_norm_w, tri, batch, seq)
    merged = _diff(proj, za, lam4, diff_subln_w, batch, seq)
    out = _tail(merged, x2, w_out[0].astype(BF16), norm_ffn_w, w_ffn_in[0].astype(BF16),
                w_ffn_out[0].astype(BF16), norm_final_w.reshape(1, d))
    return out.reshape(batch, seq, d)
```
